```python
import jax
import jax.numpy as jnp
from jax import lax
import numpy as np

D_MODEL = 1024
BATCH = 2
SEQ = 16384
DEPTH = 4
DEC_BATCH = 8
DEC_SEQ = 32
PAST_LEN = 1024

CHUNK = 64
N_EVEN = (DEPTH + 1) // 2
N_ODD = DEPTH // 2
D_FF = 2816
FFN_RES = 0.5
A_HEAD_DIM = 64
D_A = D_MODEL // 2
A_HEADS = D_A // A_HEAD_DIM
DECAY_RANK = 64
ICL_RANK = 64
GATE_RANK = 128
A_COLS = 3 * D_A + DECAY_RANK + ICL_RANK + GATE_RANK
D_B = D_MODEL - D_A
CONV_W = 3
B_COLS = 3 * D_B
EVEN_IN = A_COLS + B_COLS
D_EVEN_OUT = D_A + D_B
C_HEADS = 4
C_DK = D_MODEL // 2 // C_HEADS
C_DV = D_MODEL // C_HEADS
C_KEY = C_HEADS * C_DK
C_VAL = C_HEADS * C_DV
GLA_GATE_RANK = 16
GLA_GATE_NORM = 16.0
ODD_IN = 2 * C_KEY + 2 * C_VAL + GLA_GATE_RANK

RMS_EPS = 1e-6
GN_EPS = 64e-5
L2_EPS = 1e-12

kernel_name = 'hybrid_streaming_encoder_step'


def _rms(x, g):
    xf = x.astype(jnp.float32)
    xf = xf * lax.rsqrt(jnp.mean(xf * xf, axis=-1, keepdims=True) + RMS_EPS)
    return (xf * g.astype(jnp.float32)).astype(x.dtype)


def _swiglu(h, w_gate, w_up, w_down):
    a = jnp.einsum('btd,df->btf', h, w_gate)
    b = jnp.einsum('btd,df->btf', h, w_up)
    return jnp.einsum('btf,fd->btd', jax.nn.silu(a) * b, w_down)


def _rwkv7_scan(r, w, k, v, kk, a, s0):
    def step(s, inp):
        r_t, w_t, k_t, v_t, kk_t, a_t = inp
        sa = jnp.einsum('bhvk,bhk->bhv', s, kk_t)
        s = (s * w_t[:, :, None, :]
             - sa[..., None] * (kk_t * a_t)[:, :, None, :]
             + v_t[..., None] * k_t[:, :, None, :])
        y = jnp.einsum('bhvk,bhk->bhv', s, r_t)
        return s, y
    xs = tuple(jnp.moveaxis(z, 1, 0) for z in (r, w, k, v, kk, a))
    s_fin, ys = lax.scan(step, s0, xs)
    return jnp.moveaxis(ys, 0, 1), s_fin


def _gla_chunked(q, k, v, log_a, s0):
    bsz, t, nh, _ = q.shape
    dv = v.shape[-1]
    blk = CHUNK if t % CHUNK == 0 else t
    nc = t // blk

    def blocks(z):
        return z.reshape(bsz, nc, blk, nh, z.shape[-1]).transpose(1, 0, 3, 2, 4)

    qb, kb, vb, gb = blocks(q), blocks(k), blocks(v), blocks(log_a)
    cum = jnp.cumsum(gb, axis=3)
    last = cum[:, :, :, -1:, :]
    q_dec = qb * jnp.exp(cum)
    k_dec = kb * jnp.exp(-cum)
    k_to_end = kb * jnp.exp(last - cum)
    mask = jnp.tril(jnp.ones((blk, blk), dtype=bool))
    scores = jnp.where(mask, jnp.einsum('nbhtd,nbhsd->nbhts', q_dec, k_dec), 0.0)
    o_intra = jnp.einsum('nbhts,nbhsv->nbhtv', scores, vb)
    kv_blk = jnp.einsum('nbhsd,nbhsv->nbhdv', k_to_end, vb)
    blk_decay = jnp.exp(last[:, :, :, 0, :])

    def step(s, inp):
        q_c, kv_c, dec_c = inp
        o_inter = jnp.einsum('bhtd,bhdv->bhtv', q_c, s)
        s = s * dec_c[..., None] + kv_c
        return s, o_inter

    s_fin, o_inter = lax.scan(step, s0, (q_dec, kv_blk, blk_decay))
    o = (o_intra + o_inter).transpose(1, 0, 3, 2, 4).reshape(bsz, t, nh, dv)
    return o, s_fin


def _even_mixer(h, shift_buf, wkv0, conv_buf, p, i):
    bsz, t, _ = h.shape
    f32 = jnp.float32
    proj = jnp.einsum('btd,de->bte', h, p['ev_w_in'][i]).astype(f32)
    pa, pb = proj[..., :A_COLS], proj[..., A_COLS:]
    prev = jnp.concatenate([shift_buf.astype(f32)[:, None], pa[:, :-1]], axis=1)
    xs = pa + (prev - pa) * p['rwkv_mu'][i]
    r = xs[..., :D_A]
    k = xs[..., D_A:2 * D_A]
    v = xs[..., 2 * D_A:3 * D_A]
    o = 3 * D_A
    w_lo = xs[..., o:o + DECAY_RANK]
    o += DECAY_RANK
    a_lo = xs[..., o:o + ICL_RANK]
    o += ICL_RANK
    g_lo = xs[..., o:o + GATE_RANK]
    w_log = jax.nn.log_sigmoid(p['rwkv_w0'][i] + jnp.tanh(w_lo) @ p['rwkv_w_up'][i]) - 0.5
    decay = jnp.exp(-jnp.exp(w_log))
    a = jax.nn.sigmoid(p['rwkv_a0'][i] + a_lo @ p['rwkv_a_up'][i])
    g = jax.nn.sigmoid(g_lo) @ p['rwkv_g_up'][i]

    def heads(z):
        return z.reshape(bsz, t, A_HEADS, A_HEAD_DIM)

    kk = heads(k * p['rwkv_k_k'][i])
    kk = kk / jnp.maximum(jnp.sqrt(jnp.sum(kk * kk, axis=-1, keepdims=True)), L2_EPS)
    k_eff = heads(k * (1.0 + (a - 1.0) * p['rwkv_k_a'][i]))
    r_h, v_h = heads(r), heads(v)
    y, wkv_fin = _rwkv7_scan(r_h, heads(decay), k_eff, v_h, kk, heads(a), wkv0.astype(f32))
    mean = jnp.mean(y, axis=-1, keepdims=True)
    var = jnp.mean(jnp.square(y - mean), axis=-1, keepdims=True)
    y = ((y - mean) * lax.rsqrt(var + GN_EPS)).reshape(bsz, t, D_A) * p['rwkv_ln_w'][i] + p['rwkv_ln_b'][i]
    bonus = jnp.sum(r_h * k_eff * p['rwkv_r_k'][i], axis=-1, keepdims=True) * v_h
    y_a = (y + bonus.reshape(bsz, t, D_A)) * g
    gate_b, gate_c, hb = pb[..., :D_B], pb[..., D_B:2 * D_B], pb[..., 2 * D_B:]
    u = gate_c * hb
    padded = jnp.concatenate([conv_buf.astype(f32), u], axis=1)
    conv = lax.conv_general_dilated(padded, p['conv_w'][i].astype(f32)[:, None, :],
                                    window_strides=(1,), padding='VALID',
                                    dimension_numbers=('NWC', 'WIO', 'NWC'),
                                    feature_group_count=D_B)
    y_b = gate_b * conv
    out = jnp.einsum('bte,ed->btd', jnp.concatenate([y_a, y_b], axis=-1).astype(h.dtype), p['ev_w_out'][i])
    return out, pa[:, -1], wkv_fin, padded[:, -(CONV_W - 1):]


def _odd_mixer(h, gla0, p, i):
    bsz, t, _ = h.shape
    f32 = jnp.float32
    proj = jnp.einsum('btd,de->bte', h, p['od_w_in'][i]).astype(f32)
    q = proj[..., :C_KEY]
    k = proj[..., C_KEY:2 * C_KEY]
    v = proj[..., 2 * C_KEY:2 * C_KEY + C_VAL]
    g = proj[..., 2 * C_KEY + C_VAL:2 * C_KEY + 2 * C_VAL]
    a_lo = proj[..., 2 * C_KEY + 2 * C_VAL:]
    log_a = jax.nn.log_sigmoid(a_lo @ p['gla_a_up'][i] + p['gla_a_b'][i]) / GLA_GATE_NORM
    q = q.reshape(bsz, t, C_HEADS, C_DK) * (C_DK ** -0.5)
    k = k.reshape(bsz, t, C_HEADS, C_DK)
    v = v.reshape(bsz, t, C_HEADS, C_DV)
    log_a = log_a.reshape(bsz, t, C_HEADS, C_DK)
    o, gla_fin = _gla_chunked(q, k, v, log_a, gla0.astype(f32))
    o = o * lax.rsqrt(jnp.mean(o * o, axis=-1, keepdims=True) + RMS_EPS) * p['gla_norm'][i]
    o = o.reshape(bsz, t, C_VAL) * jax.nn.silu(g)
    out = jnp.einsum('bte,ed->btd', o.astype(h.dtype), p['od_w_out'][i])
    return out, gla_fin


def _trunk(x, shift0, wkv0, conv0, gla0, p):
    new_shift, new_wkv, new_conv, new_gla = [], [], [], []
    for layer in range(DEPTH):
        x = x + FFN_RES * _swiglu(_rms(x, p['ffn_norm'][layer, 0]), p['ffn_w_gate'][layer, 0],
                                  p['ffn_w_up'][layer, 0], p['ffn_w_down'][layer, 0])
        h = _rms(x, p['mix_norm'][layer])
        i = layer // 2
        if layer % 2 == 0:
            out, sh, wkv, cv = _even_mixer(h, shift0[i], wkv0[i], conv0[i], p, i)
            new_shift.append(sh)
            new_wkv.append(wkv)
            new_conv.append(cv)
        else:
            out, sg = _odd_mixer(h, gla0[i], p, i)
            new_gla.append(sg)
        x = x + out
        x = x + FFN_RES * _swiglu(_rms(x, p['ffn_norm'][layer, 1]), p['ffn_w_gate'][layer, 1],
                                  p['ffn_w_up'][layer, 1], p['ffn_w_down'][layer, 1])
    y = _rms(x, p['final_norm'])
    return y, jnp.stack(new_shift), jnp.stack(new_wkv), jnp.stack(new_conv), jnp.stack(new_gla)


def setup_inputs(seed: int = 0) -> dict:
    key = jax.random.key(seed)
    ks = iter(jax.random.split(key, 40))

    def nrm(shape, scale):
        return scale * jax.random.normal(next(ks), shape, jnp.float32)

    return {
        'x_prompt': nrm((BATCH, SEQ, D_MODEL), 1.0),
        'x_sample': nrm((DEC_BATCH, DEC_SEQ, D_MODEL), 1.0),
        'state_rwkv_shift': nrm((N_EVEN, DEC_BATCH, A_COLS), 1.0),
        'state_rwkv_wkv': nrm((N_EVEN, DEC_BATCH, A_HEADS, A_HEAD_DIM, A_HEAD_DIM), 0.3),
        'state_conv': nrm((N_EVEN, DEC_BATCH, CONV_W - 1, D_B), 0.5),
        'state_gla': nrm((N_ODD, DEC_BATCH, C_HEADS, C_DK, C_DV), 0.3),
        'ffn_norm': 1.0 + nrm((DEPTH, 2, D_MODEL), 0.02),
        'ffn_w_gate': nrm((DEPTH, 2, D_MODEL, D_FF), D_MODEL ** -0.5),
        'ffn_w_up': nrm((DEPTH, 2, D_MODEL, D_FF), D_MODEL ** -0.5),
        'ffn_w_down': nrm((DEPTH, 2, D_FF, D_MODEL), D_FF ** -0.5),
        'mix_norm': 1.0 + nrm((DEPTH, D_MODEL), 0.02),
        'ev_w_in': nrm((N_EVEN, D_MODEL, EVEN_IN), D_MODEL ** -0.5),
        'ev_w_out': nrm((N_EVEN, D_EVEN_OUT, D_MODEL), D_EVEN_OUT ** -0.5),
        'rwkv_mu': jax.random.uniform(next(ks), (N_EVEN, A_COLS), jnp.float32),
        'rwkv_w0': nrm((N_EVEN, D_A), 0.5),
        'rwkv_w_up': nrm((N_EVEN, DECAY_RANK, D_A), 0.3 * DECAY_RANK ** -0.5),
        'rwkv_a0': nrm((N_EVEN, D_A), 0.3),
        'rwkv_a_up': nrm((N_EVEN, ICL_RANK, D_A), 0.5 * ICL_RANK ** -0.5),
        'rwkv_g_up': nrm((N_EVEN, GATE_RANK, D_A), GATE_RANK ** -0.5),
        'rwkv_k_k': 0.85 + nrm((N_EVEN, D_A), 0.05),
        'rwkv_k_a': 1.0 + nrm((N_EVEN, D_A), 0.05),
        'rwkv_r_k': nrm((N_EVEN, A_HEADS, A_HEAD_DIM), 0.1),
        'rwkv_ln_w': 1.0 + nrm((N_EVEN, D_A), 0.02),
        'rwkv_ln_b': nrm((N_EVEN, D_A), 0.02),
        'conv_w': nrm((N_EVEN, CONV_W, D_B), CONV_W ** -0.5),
        'od_w_in': nrm((N_ODD, D_MODEL, ODD_IN), D_MODEL ** -0.5),
        'od_w_out': nrm((N_ODD, C_VAL, D_MODEL), C_VAL ** -0.5),
        'gla_a_up': nrm((N_ODD, GLA_GATE_RANK, C_KEY), GLA_GATE_RANK ** -0.5),
        'gla_a_b': nrm((N_ODD, C_KEY), 0.5),
        'gla_norm': 1.0 + nrm((N_ODD, C_DV), 0.02),
        'final_norm': 1.0 + nrm((D_MODEL,), 0.02),
    }


def reference(x_prompt, x_sample, state_rwkv_shift, state_rwkv_wkv, state_conv, state_gla,
              ffn_norm, ffn_w_gate, ffn_w_up, ffn_w_down, mix_norm,
              ev_w_in, ev_w_out, rwkv_mu, rwkv_w0, rwkv_w_up, rwkv_a0, rwkv_a_up, rwkv_g_up,
              rwkv_k_k, rwkv_k_a, rwkv_r_k, rwkv_ln_w, rwkv_ln_b, conv_w,
              od_w_in, od_w_out, gla_a_up, gla_a_b, gla_norm, final_norm):
    p = {
        'ffn_norm': ffn_norm, 'ffn_w_gate': ffn_w_gate, 'ffn_w_up': ffn_w_up, 'ffn_w_down': ffn_w_down,
        'mix_norm': mix_norm, 'ev_w_in': ev_w_in, 'ev_w_out': ev_w_out, 'rwkv_mu': rwkv_mu,
        'rwkv_w0': rwkv_w0, 'rwkv_w_up': rwkv_w_up, 'rwkv_a0': rwkv_a0, 'rwkv_a_up': rwkv_a_up,
        'rwkv_g_up': rwkv_g_up, 'rwkv_k_k': rwkv_k_k, 'rwkv_k_a': rwkv_k_a, 'rwkv_r_k': rwkv_r_k,
        'rwkv_ln_w': rwkv_ln_w, 'rwkv_ln_b': rwkv_ln_b, 'conv_w': conv_w,
        'od_w_in': od_w_in, 'od_w_out': od_w_out, 'gla_a_up': gla_a_up, 'gla_a_b': gla_a_b,
        'gla_norm': gla_norm, 'final_norm': final_norm,
    }
    bsz = x_prompt.shape[0]
    y_prompt, p_shift, p_wkv, p_conv, p_gla = _trunk(
        x_prompt,
        jnp.zeros((N_EVEN, bsz, A_COLS), jnp.float32),
        jnp.zeros((N_EVEN, bsz, A_HEADS, A_HEAD_DIM, A_HEAD_DIM), jnp.float32),
        jnp.zeros((N_EVEN, bsz, CONV_W - 1, D_B), jnp.float32),
        jnp.zeros((N_ODD, bsz, C_HEADS, C_DK, C_DV), jnp.float32),
        p)
    y_sample, s_shift, s_wkv, s_conv, s_gla = _trunk(
        x_sample, state_rwkv_shift, state_rwkv_wkv, state_conv, state_gla, p)
    return (y_prompt, y_sample, p_shift, p_wkv, p_conv, p_gla, s_shift, s_wkv, s_conv, s_gla)
```

```python
import functools

import jax
import jax.numpy as jnp
from jax import lax
from jax.experimental import pallas as pl
from jax.experimental.pallas import tpu as pltpu

F32 = jnp.float32
BF16 = jnp.bfloat16

D_MODEL = 1024
D_FF = 2816
FFN_RES = 0.5
A_HEADS, A_HEAD_DIM = 8, 64
D_A = A_HEADS * A_HEAD_DIM
DECAY_RANK, ICL_RANK, GATE_RANK = 64, 64, 128
LOW_RANK = DECAY_RANK + ICL_RANK + GATE_RANK
A_COLS = 3 * D_A + LOW_RANK
D_B = D_MODEL - D_A
CONV_W = 3
EVEN_IN = A_COLS + 3 * D_B
C_HEADS, C_DK, C_DV = 4, 128, 256
C_KEY, C_VAL = C_HEADS * C_DK, C_HEADS * C_DV
GLA_GATE_RANK = 16
GLA_GATE_NORM = 16.0
ODD_MAIN = 2 * C_KEY + 2 * C_VAL
RMS_EPS = 1e-6
GN_EPS = 64e-5
L2_EPS = 1e-12

LANES = 128
FF_CHUNK = 256
ODD_IN_PAD = ODD_MAIN + LANES
VMEM_LIMIT = 56 * 1024 * 1024


def _dot(a, b):
    return jnp.dot(a, b, preferred_element_type=F32)


def _dot_nt(a, b):
    return lax.dot_general(a, b, (((1,), (1,)), ((), ())), preferred_element_type=F32)


def _dot_tn(a, b):
    return lax.dot_general(a, b, (((0,), (0,)), ((), ())), preferred_element_type=F32)


def _split(x, parts):
    out = []
    for _ in range(parts - 1):
        p = x.astype(BF16)
        out.append(p)
        x = x - p.astype(F32)
    out.append(x.astype(BF16))
    return out


def _dot_exact_rhs(x, m_bf16, parts):
    acc = None
    for p in _split(x, parts):
        t = _dot(p, m_bf16)
        acc = t if acc is None else acc + t
    return acc


def _dot_exact_lhs(m_bf16, x, parts):
    acc = None
    for p in _split(x, parts):
        t = _dot(m_bf16, p)
        acc = t if acc is None else acc + t
    return acc


def _rms(x, g):
    return x * lax.rsqrt(jnp.mean(x * x, axis=-1, keepdims=True) + RMS_EPS) * g


def _log_sigmoid(x):
    return jnp.minimum(x, 0.0) - jnp.log(1.0 + jnp.exp(-jnp.abs(x)))


def _tril_masks(c):
    row = lax.broadcasted_iota(jnp.int32, (c, c), 0)
    col = lax.broadcasted_iota(jnp.int32, (c, c), 1)
    return row > col, row >= col


def _ffn_residual(x, g_ref, wg_ref, wu_ref, wd_ref, h_ref, acc_ref):
    h_ref[...] = _rms(x, g_ref[...]).astype(BF16)
    acc_ref[...] = jnp.zeros_like(acc_ref)

    def body(j, carry):
        h = h_ref[...]
        a = _dot(h, wg_ref[j])
        b = _dot(h, wu_ref[j])
        act = (a * jax.nn.sigmoid(a) * b).astype(BF16)
        acc_ref[...] += _dot(act, wd_ref[j])
        return carry

    lax.fori_loop(0, wg_ref.shape[0], body, 0)
    return x + FFN_RES * acc_ref[...]


def _pre_kernel(x_ref, g_ref, wg_ref, wu_ref, wd_ref, gm_ref, win_ref,
                x1_ref, proj_ref, h_ref, acc_ref):
    x1 = _ffn_residual(x_ref[...], g_ref, wg_ref, wu_ref, wd_ref, h_ref, acc_ref)
    x1_ref[...] = x1
    hm = _rms(x1, gm_ref[...]).astype(BF16)
    proj_ref[...] = _dot(hm, win_ref[...])


def _post_kernel(x_ref, y_ref, wout_ref, g_ref, wg_ref, wu_ref, wd_ref, gf_ref,
                 out_ref, h_ref, acc_ref, *, final):
    x2 = x_ref[...] + _dot(y_ref[...], wout_ref[...])
    x3 = _ffn_residual(x2, g_ref, wg_ref, wu_ref, wd_ref, h_ref, acc_ref)
    if final:
        x3 = _rms(x3, gf_ref[...])
    out_ref[...] = x3


def _resident(a):
    nd = a.ndim
    return pl.BlockSpec(a.shape, lambda *_: (0,) * nd, pipeline_mode=pl.Buffered(1))


def _row_tile(n, tm, width):
    return pl.BlockSpec((tm, width), lambda i: (i, 0))


def _pre_call(x, ffn, gm, w_in, tm):
    n = x.shape[0]
    p = w_in.shape[1]
    g, wg, wu, wd = ffn
    return pl.pallas_call(
        _pre_kernel,
        grid=(n // tm,),
        in_specs=[_row_tile(n, tm, D_MODEL)] + [_resident(a) for a in (g, wg, wu, wd, gm, w_in)],
        out_specs=[_row_tile(n, tm, D_MODEL), _row_tile(n, tm, p)],
        out_shape=[jax.ShapeDtypeStruct((n, D_MODEL), F32), jax.ShapeDtypeStruct((n, p), F32)],
        scratch_shapes=[pltpu.VMEM((tm, D_MODEL), BF16), pltpu.VMEM((tm, D_MODEL), F32)],
        compiler_params=pltpu.CompilerParams(dimension_semantics=("arbitrary",),
                                             vmem_limit_bytes=VMEM_LIMIT),
        name="pre_ffn_proj",
    )(x, g, wg, wu, wd, gm, w_in)


def _post_call(x, y, w_out, ffn, gf, tm, final):
    n = x.shape[0]
    g, wg, wu, wd = ffn
    return pl.pallas_call(
        functools.partial(_post_kernel, final=final),
        grid=(n // tm,),
        in_specs=[_row_tile(n, tm, D_MODEL), _row_tile(n, tm, D_MODEL)]
        + [_resident(a) for a in (w_out, g, wg, wu, wd, gf)],
        out_specs=_row_tile(n, tm, D_MODEL),
        out_shape=jax.ShapeDtypeStruct((n, D_MODEL), F32),
        scratch_shapes=[pltpu.VMEM((tm, D_MODEL), BF16), pltpu.VMEM((tm, D_MODEL), F32)],
        compiler_params=pltpu.CompilerParams(dimension_semantics=("arbitrary",),
                                             vmem_limit_bytes=VMEM_LIMIT),
        name="post_proj_ffn",
    )(x, y, w_out, g, wg, wu, wd, gf)


def _neumann_correction(n0, c):
    t = n0
    m = n0
    steps = c.bit_length() - 2
    for _ in range(steps):
        mb = m.astype(BF16)
        m = _dot(mb, mb)
        t = t + m + _dot(t.astype(BF16), m.astype(BF16))
    return t


def _rwkv_chunk(s_ref, r, l, k, v, kk, b, c):
    strict, incl = _tril_masks(c)
    tri = jnp.where(incl, 1.0, 0.0).astype(BF16)
    cum = _dot_exact_lhs(tri, l, 3)
    ctot = cum[c - 1:c, :]
    e_neg = jnp.exp(-cum)
    e_end = jnp.exp(ctot - cum)
    kap = kk * jnp.exp(cum - l)
    rt = r * jnp.exp(cum)
    bt = b * e_neg
    kt = k * e_neg
    b2e = b * e_end
    k2e = k * e_end
    gam = jnp.exp(ctot)
    ys = []
    for h in range(A_HEADS):
        sl = slice(h * A_HEAD_DIM, (h + 1) * A_HEAD_DIM)
        s_h = s_ref[h]
        lhs = jnp.concatenate([kap[:, sl], rt[:, sl]], axis=0).astype(BF16)
        gb = _dot_nt(lhs, bt[:, sl].astype(BF16))
        gk = _dot_nt(lhs, kt[:, sl].astype(BF16))
        a_b = jnp.where(strict, gb[:c], 0.0)
        a_k = jnp.where(strict, gk[:c], 0.0)
        q_b = jnp.where(incl, gb[c:], 0.0)
        q_k = jnp.where(incl, gk[c:], 0.0)
        p = _dot_nt(lhs, s_h.astype(BF16))
        v_h = v[:, sl]
        v_b = v_h.astype(BF16)
        w = p[:c] + _dot(a_k.astype(BF16), v_b)
        t = _neumann_correction(-a_b, c)
        u = w + _dot(t.astype(BF16), w.astype(BF16))
        u_b = u.astype(BF16)
        ys.append(p[c:] + _dot(q_k.astype(BF16), v_b) - _dot(q_b.astype(BF16), u_b))
        z = jnp.concatenate([v_h, -u], axis=0).astype(BF16)
        kall = jnp.concatenate([k2e[:, sl], b2e[:, sl]], axis=0).astype(BF16)
        s_ref[h] = s_h * gam[:, sl] + _dot_tn(z, kall)
    return jnp.concatenate(ys, axis=1)


def _even_kernel(proj_ref, shift0_ref, wkv0_ref, conv0_ref, mu_ref, w0_ref, a0_ref, wlr_ref,
                 kk_ref, ka_ref, rk_ref, lnw_ref, lnb_ref, cw_ref, bd_ref,
                 y_ref, shift_ref, wkv_ref, conv_ref,
                 s_ref, sh_ref, cv_ref, r_s, l_s, k_s, v_s, kk_s, b_s, yo_s, *, tb, c):
    t_idx = pl.program_id(1)

    @pl.when(t_idx == 0)
    def _():
        s_ref[...] = wkv0_ref[...]
        sh_ref[...] = shift0_ref[...]
        cv_ref[...] = conv0_ref[...]

    row = lax.broadcasted_iota(jnp.int32, (tb, 1), 0)
    pa = proj_ref[:, :A_COLS]
    prev = jnp.where(row == 0, sh_ref[...], pltpu.roll(pa, 1, 0))
    sh_ref[...] = pa[tb - 1:tb, :]
    xs = pa + (prev - pa) * mu_ref[...]
    r = xs[:, :D_A]
    k = xs[:, D_A:2 * D_A]
    v = xs[:, 2 * D_A:3 * D_A]
    lo = xs[:, 3 * D_A:]
    lane = lax.broadcasted_iota(jnp.int32, (1, LOW_RANK), 1)
    lo_act = jnp.where(lane < DECAY_RANK, jnp.tanh(lo),
                       jnp.where(lane < DECAY_RANK + ICL_RANK, lo, jax.nn.sigmoid(lo)))
    z = _dot(lo_act.astype(BF16), wlr_ref[...])
    w_log = _log_sigmoid(w0_ref[...] + z[:, :D_A]) - 0.5
    a = jax.nn.sigmoid(a0_ref[...] + z[:, D_A:2 * D_A])
    g = z[:, 2 * D_A:]
    bd = bd_ref[...]
    kk = k * kk_ref[...]
    kk = kk / jnp.maximum(jnp.sqrt(_dot_exact_rhs(kk * kk, bd, 2)), L2_EPS)
    k_eff = k * (1.0 + (a - 1.0) * ka_ref[...])
    r_s[...] = r
    l_s[...] = -jnp.exp(w_log)
    k_s[...] = k_eff
    v_s[...] = v
    kk_s[...] = kk
    b_s[...] = kk * a

    def chunk(ci, carry):
        sl = pl.ds(pl.multiple_of(ci * c, c), c)
        yo_s[sl, :] = _rwkv_chunk(s_ref, r_s[sl, :], l_s[sl, :], k_s[sl, :], v_s[sl, :],
                                  kk_s[sl, :], b_s[sl, :], c)
        return carry

    lax.fori_loop(0, tb // c, chunk, 0)

    y = yo_s[...]
    inv_n = 1.0 / A_HEAD_DIM
    d = y - _dot_exact_rhs(y, bd, 2) * inv_n
    var = _dot_exact_rhs(d * d, bd, 2) * inv_n
    yn = d * lax.rsqrt(var + GN_EPS) * lnw_ref[...] + lnb_ref[...]
    bonus = _dot_exact_rhs(r * k_eff * rk_ref[...], bd, 2) * v
    y_ref[:, :D_A] = ((yn + bonus) * g).astype(BF16)

    pb = proj_ref[:, A_COLS:]
    u = pb[:, D_B:2 * D_B] * pb[:, 2 * D_B:]
    u1 = jnp.where(row == 0, cv_ref[1:2, :], pltpu.roll(u, 1, 0))
    u2 = jnp.where(row == 0, cv_ref[0:1, :],
                   jnp.where(row == 1, cv_ref[1:2, :], pltpu.roll(u, 2, 0)))
    conv = cw_ref[0:1, :] * u2 + cw_ref[1:2, :] * u1 + cw_ref[2:3, :] * u
    cv_ref[...] = u[tb - 2:tb, :]
    y_ref[:, D_A:] = (pb[:, :D_B] * conv).astype(BF16)

    @pl.when(t_idx == pl.num_programs(1) - 1)
    def _():
        shift_ref[...] = sh_ref[...]
        wkv_ref[...] = s_ref[...]
        conv_ref[...] = cv_ref[...]


def _even_call(proj, shift0, wkv0, conv0, prm, tb, c):
    bsz, t, _ = proj.shape
    params = (prm["mu"], prm["w0"], prm["a0"], prm["w_lr"], prm["k_k"], prm["k_a"], prm["r_k"],
              prm["ln_w"], prm["ln_b"], prm["conv_w"], prm["bd"])
    per_b = lambda *tail: pl.BlockSpec((None,) + tail, lambda b, i: (b,) + (0,) * len(tail))
    state_specs = [per_b(1, A_COLS), per_b(A_HEADS, A_HEAD_DIM, A_HEAD_DIM), per_b(CONV_W - 1, D_B)]
    vec = lambda: pltpu.VMEM((tb, D_A), F32)
    return pl.pallas_call(
        functools.partial(_even_kernel, tb=tb, c=c),
        grid=(bsz, t // tb),
        in_specs=[pl.BlockSpec((None, tb, EVEN_IN), lambda b, i: (b, i, 0))] + state_specs
        + [pl.BlockSpec(a.shape, lambda b, i, nd=a.ndim: (0,) * nd) for a in params],
        out_specs=[pl.BlockSpec((None, tb, D_MODEL), lambda b, i: (b, i, 0))] + state_specs,
        out_shape=[jax.ShapeDtypeStruct((bsz, t, D_MODEL), BF16),
                   jax.ShapeDtypeStruct((bsz, 1, A_COLS), F32),
                   jax.ShapeDtypeStruct((bsz, A_HEADS, A_HEAD_DIM, A_HEAD_DIM), F32),
                   jax.ShapeDtypeStruct((bsz, CONV_W - 1, D_B), F32)],
        scratch_shapes=[pltpu.VMEM((A_HEADS, A_HEAD_DIM, A_HEAD_DIM), F32),
                        pltpu.VMEM((1, A_COLS), F32),
                        pltpu.VMEM((CONV_W - 1, D_B), F32)] + [vec() for _ in range(7)],
        compiler_params=pltpu.CompilerParams(dimension_semantics=("arbitrary", "arbitrary"),
                                             vmem_limit_bytes=VMEM_LIMIT),
        name="even_mixer",
    )(proj, shift0, wkv0, conv0, *params)


def _gla_chunk(st_ref, q, k, v, la, g, gn, c):
    _, incl = _tril_masks(c)
    tri = jnp.where(incl, 1.0, 0.0).astype(BF16)
    cum = _dot_exact_lhs(tri, la, 3)
    ctot = cum[c - 1:c, :]
    qd = q * jnp.exp(cum)
    kd = k * jnp.exp(-cum)
    k2e = k * jnp.exp(ctot - cum)
    gam = jnp.exp(ctot)
    outs = []
    for h in range(C_HEADS):
        ks = slice(h * C_DK, (h + 1) * C_DK)
        vs = slice(h * C_DV, (h + 1) * C_DV)
        st = st_ref[h]
        qd_b = qd[:, ks].astype(BF16)
        v_b = v[:, vs].astype(BF16)
        scores = jnp.where(incl, _dot_nt(qd_b, kd[:, ks].astype(BF16)), 0.0)
        o = _dot(scores.astype(BF16), v_b) + _dot_nt(qd_b, st.astype(BF16))
        st_ref[h] = st * gam[:, ks] + _dot_tn(v_b, k2e[:, ks].astype(BF16))
        o = o * lax.rsqrt(jnp.mean(o * o, axis=-1, keepdims=True) + RMS_EPS) * gn
        g_h = g[:, vs]
        outs.append(o * (g_h * jax.nn.sigmoid(g_h)))
    return jnp.concatenate(outs, axis=1)


def _odd_kernel(proj_ref, gla0_ref, aup_ref, ab_ref, gn_ref, y_ref, gla_ref,
                st_ref, la_s, *, tb, c):
    t_idx = pl.program_id(1)

    @pl.when(t_idx == 0)
    def _():
        st_ref[...] = gla0_ref[...]

    a_lo = proj_ref[:, ODD_MAIN:]
    la_s[...] = _log_sigmoid(_dot(a_lo.astype(BF16), aup_ref[...]) + ab_ref[...]) * (1.0 / GLA_GATE_NORM)
    gn = gn_ref[...]
    scale = C_DK ** -0.5

    def chunk(ci, carry):
        sl = pl.ds(pl.multiple_of(ci * c, c), c)
        q = proj_ref[sl, :C_KEY] * scale
        k = proj_ref[sl, C_KEY:2 * C_KEY]
        v = proj_ref[sl, 2 * C_KEY:2 * C_KEY + C_VAL]
        g = proj_ref[sl, 2 * C_KEY + C_VAL:ODD_MAIN]
        y_ref[sl, :] = _gla_chunk(st_ref, q, k, v, la_s[sl, :], g, gn, c).astype(BF16)
        return carry

    lax.fori_loop(0, tb // c, chunk, 0)

    @pl.when(t_idx == pl.num_programs(1) - 1)
    def _():
        gla_ref[...] = st_ref[...]


def _odd_call(proj, gla0_t, prm, tb, c):
    bsz, t, _ = proj.shape
    params = (prm["a_up"], prm["a_b"], prm["gn"])
    state_spec = pl.BlockSpec((None, C_HEADS, C_DV, C_DK), lambda b, i: (b, 0, 0, 0))
    return pl.pallas_call(
        functools.partial(_odd_kernel, tb=tb, c=c),
        grid=(bsz, t // tb),
        in_specs=[pl.BlockSpec((None, tb, ODD_IN_PAD), lambda b, i: (b, i, 0)), state_spec]
        + [pl.BlockSpec(a.shape, lambda b, i, nd=a.ndim: (0,) * nd) for a in params],
        out_specs=[pl.BlockSpec((None, tb, D_MODEL), lambda b, i: (b, i, 0)), state_spec],
        out_shape=[jax.ShapeDtypeStruct((bsz, t, D_MODEL), BF16),
                   jax.ShapeDtypeStruct((bsz, C_HEADS, C_DV, C_DK), F32)],
        scratch_shapes=[pltpu.VMEM((C_HEADS, C_DV, C_DK), F32), pltpu.VMEM((tb, C_KEY), F32)],
        compiler_params=pltpu.CompilerParams(dimension_semantics=("arbitrary", "arbitrary"),
                                             vmem_limit_bytes=VMEM_LIMIT),
        name="odd_mixer",
    )(proj, gla0_t, *params)


def _prepare(w):
    depth = w["ffn_norm"].shape[0]
    n_ff = D_FF // FF_CHUNK
    row = lambda a: a.reshape(1, -1).astype(F32)

    def ffn(layer, j):
        wg = w["ffn_w_gate"][layer, j].astype(BF16).reshape(D_MODEL, n_ff, FF_CHUNK).transpose(1, 0, 2)
        wu = w["ffn_w_up"][layer, j].astype(BF16).reshape(D_MODEL, n_ff, FF_CHUNK).transpose(1, 0, 2)
        wd = w["ffn_w_down"][layer, j].astype(BF16).reshape(n_ff, FF_CHUNK, D_MODEL)
        return row(w["ffn_norm"][layer, j]), wg, wu, wd

    head = jnp.arange(D_A) // A_HEAD_DIM
    bd = (head[:, None] == head[None, :]).astype(BF16)
    layers = []
    for layer in range(depth):
        i = layer // 2
        if layer % 2 == 0:
            w_lr = jnp.zeros((LOW_RANK, 3 * D_A), F32)
            w_lr = w_lr.at[:DECAY_RANK, :D_A].set(w["rwkv_w_up"][i])
            w_lr = w_lr.at[DECAY_RANK:DECAY_RANK + ICL_RANK, D_A:2 * D_A].set(w["rwkv_a_up"][i])
            w_lr = w_lr.at[DECAY_RANK + ICL_RANK:, 2 * D_A:].set(w["rwkv_g_up"][i])
            mix = dict(mu=row(w["rwkv_mu"][i]), w0=row(w["rwkv_w0"][i]), a0=row(w["rwkv_a0"][i]),
                       w_lr=w_lr.astype(BF16), k_k=row(w["rwkv_k_k"][i]), k_a=row(w["rwkv_k_a"][i]),
                       r_k=row(w["rwkv_r_k"][i]), ln_w=row(w["rwkv_ln_w"][i]), ln_b=row(w["rwkv_ln_b"][i]),
                       conv_w=w["conv_w"][i].astype(F32), bd=bd)
            w_in = w["ev_w_in"][i].astype(BF16)
            w_out = w["ev_w_out"][i].astype(BF16)
        else:
            a_up = jnp.zeros((LANES, C_KEY), F32).at[:GLA_GATE_RANK].set(w["gla_a_up"][i])
            mix = dict(a_up=a_up.astype(BF16), a_b=row(w["gla_a_b"][i]), gn=row(w["gla_norm"][i]))
            w_in = jnp.pad(w["od_w_in"][i], ((0, 0), (0, ODD_IN_PAD - w["od_w_in"].shape[-1]))).astype(BF16)
            w_out = w["od_w_out"][i].astype(BF16)
        layers.append(dict(ffn0=ffn(layer, 0), ffn1=ffn(layer, 1), gm=row(w["mix_norm"][layer]),
                           w_in=w_in, w_out=w_out, mix=mix))
    return layers, row(w["final_norm"])


def _trunk(x, shift0, wkv0, conv0, gla0, layers, final_norm, tm, tb, c):
    bsz, t, _ = x.shape
    n = bsz * t
    x = x.reshape(n, D_MODEL)
    new_shift, new_wkv, new_conv, new_gla = [], [], [], []
    for layer, lw in enumerate(layers):
        i = layer // 2
        x, proj = _pre_call(x, lw["ffn0"], lw["gm"], lw["w_in"], tm)
        proj = proj.reshape(bsz, t, -1)
        if layer % 2 == 0:
            mix, sh, wkv, cv = _even_call(proj, shift0[i][:, None, :], wkv0[i], conv0[i], lw["mix"], tb, c)
            new_shift.append(sh[:, 0, :])
            new_wkv.append(wkv)
            new_conv.append(cv)
        else:
            mix, st = _odd_call(proj, jnp.swapaxes(gla0[i], -1, -2), lw["mix"], tb, c)
            new_gla.append(jnp.swapaxes(st, -1, -2))
        x = _post_call(x, mix.reshape(n, D_MODEL), lw["w_out"], lw["ffn1"], final_norm, tm,
                       final=layer == len(layers) - 1)
    return (x.reshape(bsz, t, D_MODEL), jnp.stack(new_shift), jnp.stack(new_wkv),
            jnp.stack(new_conv), jnp.stack(new_gla))


def _tiles(bsz, t):
    n = bsz * t
    tm = 256 if n % 256 == 0 else n
    tb = 256 if t % 256 == 0 else t
    c = 64 if tb % 64 == 0 else tb
    return tm, tb, c


def kernel(x_prompt, x_sample, state_rwkv_shift, state_rwkv_wkv, state_conv, state_gla, ffn_norm, ffn_w_gate, ffn_w_up, ffn_w_down, mix_norm, ev_w_in, ev_w_out, rwkv_mu, rwkv_w0, rwkv_w_up, rwkv_a0, rwkv_a_up, rwkv_g_up, rwkv_k_k, rwkv_k_a, rwkv_r_k, rwkv_ln_w, rwkv_ln_b, conv_w, od_w_in, od_w_out, gla_a_up, gla_a_b, gla_norm, final_norm):
    w = dict(ffn_norm=ffn_norm, ffn_w_gate=ffn_w_gate, ffn_w_up=ffn_w_up, ffn_w_down=ffn_w_down,
             mix_norm=mix_norm, ev_w_in=ev_w_in, ev_w_out=ev_w_out, rwkv_mu=rwkv_mu, rwkv_w0=rwkv_w0,
             rwkv_w_up=rwkv_w_up, rwkv_a0=rwkv_a0, rwkv_a_up=rwkv_a_up, rwkv_g_up=rwkv_g_up,
             rwkv_k_k=rwkv_k_k, rwkv_k_a=rwkv_k_a, rwkv_r_k=rwkv_r_k.reshape(rwkv_r_k.shape[0], -1),
             rwkv_ln_w=rwkv_ln_w, rwkv_ln_b=rwkv_ln_b, conv_w=conv_w, od_w_in=od_w_in,
             od_w_out=od_w_out, gla_a_up=gla_a_up, gla_a_b=gla_a_b, gla_norm=gla_norm,
             final_norm=final_norm)
    layers, fnorm = _prepare(w)
    n_even, n_odd = state_rwkv_shift.shape[0], state_gla.shape[0]
    bp = x_prompt.shape[0]
    zeros = lambda *s: jnp.zeros(s, F32)
    y_p, p_shift, p_wkv, p_conv, p_gla = _trunk(
        x_prompt, zeros(n_even, bp, A_COLS), zeros(n_even, bp, A_HEADS, A_HEAD_DIM, A_HEAD_DIM),
        zeros(n_even, bp, CONV_W - 1, D_B), zeros(n_odd, bp, C_HEADS, C_DK, C_DV),
        layers, fnorm, *_tiles(*x_prompt.shape[:2]))
    y_s, s_shift, s_wkv, s_conv, s_gla = _trunk(
        x_sample, state_rwkv_shift, state_rwkv_wkv, state_conv, state_gla,
        layers, fnorm, *_tiles(*x_sample.shape[:2]))
    return (y_p, y_s, p_shift, p_wkv, p_conv, p_gla, s_shift, s_wkv, s_conv, s_gla)
```

```python
import functools

import jax
import jax.numpy as jnp
from jax import lax
from jax.experimental import pallas as pl
from jax.experimental.pallas import tpu as pltpu

F32 = jnp.float32
BF16 = jnp.bfloat16

D_MODEL = 1024
D_FF = 2816
FFN_RES = 0.5
A_HEADS, A_HEAD_DIM = 8, 64
D_A = A_HEADS * A_HEAD_DIM
DECAY_RANK, ICL_RANK, GATE_RANK = 64, 64, 128
LOW_RANK = DECAY_RANK + ICL_RANK + GATE_RANK
A_COLS = 3 * D_A + LOW_RANK
D_B = D_MODEL - D_A
CONV_W = 3
EVEN_IN = A_COLS + 3 * D_B
C_HEADS, C_DK, C_DV = 4, 128, 256
C_KEY, C_VAL = C_HEADS * C_DK, C_HEADS * C_DV
GLA_GATE_RANK = 16
GLA_GATE_NORM = 16.0
ODD_MAIN = 2 * C_KEY + 2 * C_VAL
RMS_EPS = 1e-6
GN_EPS = 64e-5
L2_EPS = 1e-12
DECAY_SCALE = 0.6065306597126334

LANES = 128
MXU_WIDTH = 256
FF_CHUNK = MXU_WIDTH
ODD_IN_PAD = ODD_MAIN + LANES
VMEM_LIMIT = 56 * 1024 * 1024


def _dot(a, b):
    return jnp.dot(a, b, preferred_element_type=F32)


def _dot_nt(a, b):
    return lax.dot_general(a, b, (((1,), (1,)), ((), ())), preferred_element_type=F32)


def _dot_tn(a, b):
    return lax.dot_general(a, b, (((0,), (0,)), ((), ())), preferred_element_type=F32)


def _split(x, parts):
    out = []
    for _ in range(parts - 1):
        p = x.astype(BF16)
        out.append(p)
        x = x - p.astype(F32)
    out.append(x.astype(BF16))
    return out


def _cumsum_rows(tri_bf16, x):
    acc = None
    for p in _split(x, 3):
        t = _dot(tri_bf16, p)
        acc = t if acc is None else acc + t
    return acc


def _head_sums(x, bd_bf16, parts):
    w = bd_bf16.shape[0]
    pieces = _split(x, parts)
    tiles = []
    for j in range(x.shape[1] // w):
        acc = None
        for p in pieces:
            t = _dot(p[:, j * w:(j + 1) * w], bd_bf16)
            acc = t if acc is None else acc + t
        tiles.append(acc)
    return jnp.concatenate(tiles, axis=1)


def _rms(x, g):
    return x * lax.rsqrt(jnp.mean(x * x, axis=-1, keepdims=True) + RMS_EPS) * g


def _log_sigmoid(x):
    return jnp.minimum(x, 0.0) - jnp.log(1.0 + jnp.exp(-jnp.abs(x)))


def _tril_masks(c):
    row = lax.broadcasted_iota(jnp.int32, (c, c), 0)
    col = lax.broadcasted_iota(jnp.int32, (c, c), 1)
    return row > col, row >= col


def _ffn_residual(x, g_ref, wg_ref, wu_ref, wd_ref):
    h = _rms(x, g_ref[...]).astype(BF16)
    acc = None
    for j in range(D_FF // FF_CHUNK):
        cols = slice(j * FF_CHUNK, (j + 1) * FF_CHUNK)
        a = _dot(h, wg_ref[:, cols])
        b = _dot(h, wu_ref[:, cols])
        act = (a * jax.nn.sigmoid(a) * b).astype(BF16)
        d = _dot(act, wd_ref[cols, :])
        acc = d if acc is None else acc + d
    return x + FFN_RES * acc


def _pre_kernel(x_ref, g_ref, wg_ref, wu_ref, wd_ref, gm_ref, win_ref, x1_ref, proj_ref):
    x1 = _ffn_residual(x_ref[...], g_ref, wg_ref, wu_ref, wd_ref)
    x1_ref[...] = x1
    hm = _rms(x1, gm_ref[...]).astype(BF16)
    proj_ref[...] = _dot(hm, win_ref[...])


def _post_kernel(x_ref, y_ref, wout_ref, g_ref, wg_ref, wu_ref, wd_ref, gf_ref, out_ref, *, final):
    x2 = x_ref[...] + _dot(y_ref[...], wout_ref[...])
    x3 = _ffn_residual(x2, g_ref, wg_ref, wu_ref, wd_ref)
    if final:
        x3 = _rms(x3, gf_ref[...])
    out_ref[...] = x3


def _resident(a):
    nd = a.ndim
    return pl.BlockSpec(a.shape, lambda *_: (0,) * nd, pipeline_mode=pl.Buffered(1))


def _row_tile(tm, width):
    return pl.BlockSpec((tm, width), lambda i: (i, 0))


def _pre_call(x, ffn, gm, w_in, tm):
    n = x.shape[0]
    p = w_in.shape[1]
    g, wg, wu, wd = ffn
    return pl.pallas_call(
        _pre_kernel,
        grid=(n // tm,),
        in_specs=[_row_tile(tm, D_MODEL)] + [_resident(a) for a in (g, wg, wu, wd, gm, w_in)],
        out_specs=[_row_tile(tm, D_MODEL), _row_tile(tm, p)],
        out_shape=[jax.ShapeDtypeStruct((n, D_MODEL), F32), jax.ShapeDtypeStruct((n, p), F32)],
        compiler_params=pltpu.CompilerParams(dimension_semantics=("arbitrary",),
                                             vmem_limit_bytes=VMEM_LIMIT),
        name="pre_ffn_proj",
    )(x, g, wg, wu, wd, gm, w_in)


def _post_call(x, y, w_out, ffn, gf, tm, final):
    n = x.shape[0]
    g, wg, wu, wd = ffn
    return pl.pallas_call(
        functools.partial(_post_kernel, final=final),
        grid=(n // tm,),
        in_specs=[_row_tile(tm, D_MODEL), _row_tile(tm, D_MODEL)]
        + [_resident(a) for a in (w_out, g, wg, wu, wd, gf)],
        out_specs=_row_tile(tm, D_MODEL),
        out_shape=jax.ShapeDtypeStruct((n, D_MODEL), F32),
        compiler_params=pltpu.CompilerParams(dimension_semantics=("arbitrary",),
                                             vmem_limit_bytes=VMEM_LIMIT),
        name="post_proj_ffn",
    )(x, y, w_out, g, wg, wu, wd, gf)


def _rwkv_chunk(s_ref, ins, c):
    strict, incl = _tril_masks(c)
    tri = jnp.where(incl, 1.0, 0.0).astype(BF16)
    units, kap, rt, bt, kt, b2e, k2e, gam, vv = [], [], [], [], [], [], [], [], []
    for i, (r, l, k, v, kk, b) in enumerate(ins):
        cum = _cumsum_rows(tri, l)
        ctot = cum[c - 1:c, :]
        e_neg = jnp.exp(-cum)
        e_end = jnp.exp(ctot - cum)
        kap.append(kk * jnp.exp(cum - l))
        rt.append(r * jnp.exp(cum))
        bt.append(b * e_neg)
        kt.append(k * e_neg)
        b2e.append(b * e_end)
        k2e.append(k * e_end)
        gam.append(jnp.exp(ctot))
        vv.append(v)
        units += [(i, h, slice(h * A_HEAD_DIM, (h + 1) * A_HEAD_DIM)) for h in range(A_HEADS)]
    n = range(len(units))
    s0 = [s_ref[i, h] for i, h, _ in units]
    lhs = [jnp.concatenate([kap[i][:, sl], rt[i][:, sl]], axis=0).astype(BF16) for i, _, sl in units]
    gb = [_dot_nt(lhs[j], bt[i][:, sl].astype(BF16)) for j, (i, _, sl) in enumerate(units)]
    gk = [_dot_nt(lhs[j], kt[i][:, sl].astype(BF16)) for j, (i, _, sl) in enumerate(units)]
    p = [_dot_nt(lhs[j], s0[j].astype(BF16)) for j in n]
    v_b = [vv[i][:, sl].astype(BF16) for i, _, sl in units]
    a_kq = [jnp.concatenate([jnp.where(strict, gk[j][:c], 0.0), jnp.where(incl, gk[j][c:], 0.0)],
                            axis=0).astype(BF16) for j in n]
    kv = [_dot(a_kq[j], v_b[j]) for j in n]
    t = [jnp.where(strict, -gb[j][:c], 0.0) for j in n]
    m = [_dot(x.astype(BF16), x.astype(BF16)) for x in t]
    for _ in range(c.bit_length() - 3):
        mb = [x.astype(BF16) for x in m]
        m_next = [_dot(x, x) for x in mb]
        t = [t[j] + m[j] + _dot(t[j].astype(BF16), mb[j]) for j in n]
        m = m_next
    t = [t[j] + m[j] + _dot(t[j].astype(BF16), m[j].astype(BF16)) for j in n]
    w = [p[j][:c] + kv[j][:c] for j in n]
    u = [w[j] + _dot(t[j].astype(BF16), w[j].astype(BF16)) for j in n]
    q_b = [jnp.where(incl, gb[j][c:], 0.0).astype(BF16) for j in n]
    ys = [p[j][c:] + kv[j][c:] - _dot(q_b[j], u[j].astype(BF16)) for j in n]
    for j, (i, h, sl) in enumerate(units):
        z = jnp.concatenate([vv[i][:, sl], -u[j]], axis=0).astype(BF16)
        kall = jnp.concatenate([k2e[i][:, sl], b2e[i][:, sl]], axis=0).astype(BF16)
        s_ref[i, h] = s0[j] * gam[i][:, sl] + _dot_tn(z, kall)
    return [jnp.concatenate(ys[i * A_HEADS:(i + 1) * A_HEADS], axis=1) for i in range(len(ins))]


def _even_kernel(proj_ref, shift0_ref, wkv0_ref, conv0_ref, mu_ref, w0_ref, a0_ref, wlr_ref,
                 kk_ref, ka_ref, rk_ref, lnw_ref, lnb_ref, cw_ref, bd_ref,
                 y_ref, shift_ref, wkv_ref, conv_ref,
                 sh_ref, s_ref, cv_ref, r_s, l_s, k_s, v_s, kk_s, b_s, g_s, yo_s, *, bg, tb, c):
    t_idx = pl.program_id(1)

    @pl.when(t_idx == 0)
    def _():
        s_ref[...] = wkv0_ref[...]
        sh_ref[...] = shift0_ref[...]
        cv_ref[...] = conv0_ref[...]

    row = lax.broadcasted_iota(jnp.int32, (tb, 1), 0)
    lane = lax.broadcasted_iota(jnp.int32, (1, LOW_RANK), 1)
    bd = bd_ref[...]
    for i in range(bg):
        pa = proj_ref[i, :, :A_COLS]
        prev = jnp.where(row == 0, sh_ref[i], pltpu.roll(pa, 1, 0))
        sh_ref[i] = pa[tb - 1:tb, :]
        xs = pa + (prev - pa) * mu_ref[...]
        r = xs[:, :D_A]
        k = xs[:, D_A:2 * D_A]
        lo = xs[:, 3 * D_A:]
        lo_act = jnp.where(lane < DECAY_RANK, jnp.tanh(lo),
                           jnp.where(lane < DECAY_RANK + ICL_RANK, lo, jax.nn.sigmoid(lo)))
        z = _dot(lo_act.astype(BF16), wlr_ref[...])
        l_s[i] = jax.nn.sigmoid(w0_ref[...] + z[:, :D_A]) * (-DECAY_SCALE)
        a = jax.nn.sigmoid(a0_ref[...] + z[:, D_A:2 * D_A])
        g_s[i] = z[:, 2 * D_A:]
        kk = k * kk_ref[...]
        kk = kk * jnp.minimum(lax.rsqrt(_head_sums(kk * kk, bd, 2)), 1.0 / L2_EPS)
        r_s[i] = r
        k_s[i] = k * (1.0 + (a - 1.0) * ka_ref[...])
        v_s[i] = xs[:, 2 * D_A:3 * D_A]
        kk_s[i] = kk
        b_s[i] = kk * a

    def chunk(ci, carry):
        sl = pl.ds(pl.multiple_of(ci * c, c), c)
        ins = [tuple(ref[i, sl, :] for ref in (r_s, l_s, k_s, v_s, kk_s, b_s)) for i in range(bg)]
        for i, y in enumerate(_rwkv_chunk(s_ref, ins, c)):
            yo_s[i, sl, :] = y
        return carry

    lax.fori_loop(0, tb // c, chunk, 0)

    inv_n = 1.0 / A_HEAD_DIM
    for i in range(bg):
        y = yo_s[i]
        d = y - _head_sums(y, bd, 2) * inv_n
        var = _head_sums(d * d, bd, 1) * inv_n
        yn = d * lax.rsqrt(var + GN_EPS) * lnw_ref[...] + lnb_ref[...]
        bonus = _head_sums(r_s[i] * k_s[i] * rk_ref[...], bd, 2) * v_s[i]
        y_ref[i, :, :D_A] = ((yn + bonus) * g_s[i]).astype(BF16)

        pb = proj_ref[i, :, A_COLS:]
        u = pb[:, D_B:2 * D_B] * pb[:, 2 * D_B:]
        cv = cv_ref[i]
        u1 = jnp.where(row == 0, cv[1:2, :], pltpu.roll(u, 1, 0))
        u2 = jnp.where(row == 0, cv[0:1, :], jnp.where(row == 1, cv[1:2, :], pltpu.roll(u, 2, 0)))
        conv = cw_ref[0:1, :] * u2 + cw_ref[1:2, :] * u1 + cw_ref[2:3, :] * u
        cv_ref[i] = u[tb - 2:tb, :]
        y_ref[i, :, D_A:] = (pb[:, :D_B] * conv).astype(BF16)

    @pl.when(t_idx == pl.num_programs(1) - 1)
    def _():
        shift_ref[...] = sh_ref[...]
        wkv_ref[...] = s_ref[...]
        conv_ref[...] = cv_ref[...]


def _even_call(proj, shift0, wkv0, conv0, prm, bg, tb, c):
    bsz, t, _ = proj.shape
    params = (prm["mu"], prm["w0"], prm["a0"], prm["w_lr"], prm["k_k"], prm["k_a"], prm["r_k"],
              prm["ln_w"], prm["ln_b"], prm["conv_w"], prm["bd"])
    per_b = lambda *tail: pl.BlockSpec((bg,) + tail, lambda b, i: (b,) + (0,) * len(tail))
    state_shapes = [(1, A_COLS), (A_HEADS, A_HEAD_DIM, A_HEAD_DIM), (CONV_W - 1, D_B)]
    state_specs = [per_b(*s) for s in state_shapes]
    return pl.pallas_call(
        functools.partial(_even_kernel, bg=bg, tb=tb, c=c),
        grid=(bsz // bg, t // tb),
        in_specs=[pl.BlockSpec((bg, tb, EVEN_IN), lambda b, i: (b, i, 0))] + state_specs
        + [pl.BlockSpec(a.shape, lambda b, i, nd=a.ndim: (0,) * nd) for a in params],
        out_specs=[pl.BlockSpec((bg, tb, D_MODEL), lambda b, i: (b, i, 0))] + state_specs,
        out_shape=[jax.ShapeDtypeStruct((bsz, t, D_MODEL), BF16)]
        + [jax.ShapeDtypeStruct((bsz,) + s, F32) for s in state_shapes],
        scratch_shapes=[pltpu.VMEM((bg,) + s, F32) for s in state_shapes]
        + [pltpu.VMEM((bg, tb, D_A), F32) for _ in range(8)],
        compiler_params=pltpu.CompilerParams(dimension_semantics=("arbitrary", "arbitrary"),
                                             vmem_limit_bytes=VMEM_LIMIT),
        name="even_mixer",
    )(proj, shift0, wkv0, conv0, *params)


def _gla_chunk(st_ref, ins, gn, c):
    _, incl = _tril_masks(c)
    tri = jnp.where(incl, 1.0, 0.0).astype(BF16)
    units, qd, kd, k2e, gam = [], [], [], [], []
    for i, (q, k, v, la, g) in enumerate(ins):
        cum = _cumsum_rows(tri, la)
        ctot = cum[c - 1:c, :]
        qd.append((q * jnp.exp(cum)).astype(BF16))
        kd.append((k * jnp.exp(-cum)).astype(BF16))
        k2e.append((k * jnp.exp(ctot - cum)).astype(BF16))
        gam.append(jnp.exp(ctot))
        units += [(i, h, slice(h * C_DK, (h + 1) * C_DK), slice(h * C_DV, (h + 1) * C_DV))
                  for h in range(C_HEADS)]
    n = range(len(units))
    st = [st_ref[i, h] for i, h, _, _ in units]
    v_b = [ins[i][2][:, vs].astype(BF16) for i, _, _, vs in units]
    scores = [jnp.where(incl, _dot_nt(qd[i][:, ks], kd[i][:, ks]), 0.0).astype(BF16)
              for i, _, ks, _ in units]
    inter = [_dot_nt(qd[i][:, ks], st[j].astype(BF16)) for j, (i, _, ks, _) in enumerate(units)]
    upd = [_dot_tn(v_b[j], k2e[i][:, ks]) for j, (i, _, ks, _) in enumerate(units)]
    o = [_dot(scores[j], v_b[j]) + inter[j] for j in n]
    outs = []
    for j, (i, h, ks, vs) in enumerate(units):
        st_ref[i, h] = st[j] * gam[i][:, ks] + upd[j]
        on = o[j] * lax.rsqrt(jnp.mean(o[j] * o[j], axis=-1, keepdims=True) + RMS_EPS) * gn
        g_h = ins[i][4][:, vs]
        outs.append(on * (g_h * jax.nn.sigmoid(g_h)))
    return [jnp.concatenate(outs[i * C_HEADS:(i + 1) * C_HEADS], axis=1) for i in range(len(ins))]


def _odd_kernel(proj_ref, gla0_ref, aup_ref, ab_ref, gn_ref, y_ref, gla_ref,
                st_ref, la_s, *, bg, tb, c):
    t_idx = pl.program_id(1)

    @pl.when(t_idx == 0)
    def _():
        st_ref[...] = gla0_ref[...]

    for i in range(bg):
        a_lo = proj_ref[i, :, ODD_MAIN:]
        la_s[i] = _log_sigmoid(_dot(a_lo.astype(BF16), aup_ref[...]) + ab_ref[...]) * (1.0 / GLA_GATE_NORM)
    gn = gn_ref[...]
    scale = C_DK ** -0.5

    def chunk(ci, carry):
        sl = pl.ds(pl.multiple_of(ci * c, c), c)
        ins = [(proj_ref[i, sl, :C_KEY] * scale,
                proj_ref[i, sl, C_KEY:2 * C_KEY],
                proj_ref[i, sl, 2 * C_KEY:2 * C_KEY + C_VAL],
                la_s[i, sl, :],
                proj_ref[i, sl, 2 * C_KEY + C_VAL:ODD_MAIN]) for i in range(bg)]
        for i, y in enumerate(_gla_chunk(st_ref, ins, gn, c)):
            y_ref[i, sl, :] = y.astype(BF16)
        return carry

    lax.fori_loop(0, tb // c, chunk, 0)

    @pl.when(t_idx == pl.num_programs(1) - 1)
    def _():
        gla_ref[...] = st_ref[...]


def _odd_call(proj, gla0_t, prm, bg, tb, c):
    bsz, t, _ = proj.shape
    params = (prm["a_up"], prm["a_b"], prm["gn"])
    state_spec = pl.BlockSpec((bg, C_HEADS, C_DV, C_DK), lambda b, i: (b, 0, 0, 0))
    return pl.pallas_call(
        functools.partial(_odd_kernel, bg=bg, tb=tb, c=c),
        grid=(bsz // bg, t // tb),
        in_specs=[pl.BlockSpec((bg, tb, ODD_IN_PAD), lambda b, i: (b, i, 0)), state_spec]
        + [pl.BlockSpec(a.shape, lambda b, i, nd=a.ndim: (0,) * nd) for a in params],
        out_specs=[pl.BlockSpec((bg, tb, D_MODEL), lambda b, i: (b, i, 0)), state_spec],
        out_shape=[jax.ShapeDtypeStruct((bsz, t, D_MODEL), BF16),
                   jax.ShapeDtypeStruct((bsz, C_HEADS, C_DV, C_DK), F32)],
        scratch_shapes=[pltpu.VMEM((bg, C_HEADS, C_DV, C_DK), F32), pltpu.VMEM((bg, tb, C_KEY), F32)],
        compiler_params=pltpu.CompilerParams(dimension_semantics=("arbitrary", "arbitrary"),
                                             vmem_limit_bytes=VMEM_LIMIT),
        name="odd_mixer",
    )(proj, gla0_t, *params)


def _prepare(w):
    depth = w["ffn_norm"].shape[0]
    row = lambda a: a.reshape(1, -1).astype(F32)

    def ffn(layer, j):
        return (row(w["ffn_norm"][layer, j]), w["ffn_w_gate"][layer, j].astype(BF16),
                w["ffn_w_up"][layer, j].astype(BF16), w["ffn_w_down"][layer, j].astype(BF16))

    head = jnp.arange(MXU_WIDTH) // A_HEAD_DIM
    bd = (head[:, None] == head[None, :]).astype(BF16)
    layers = []
    for layer in range(depth):
        i = layer // 2
        if layer % 2 == 0:
            w_lr = jnp.zeros((LOW_RANK, 3 * D_A), F32)
            w_lr = w_lr.at[:DECAY_RANK, :D_A].set(w["rwkv_w_up"][i])
            w_lr = w_lr.at[DECAY_RANK:DECAY_RANK + ICL_RANK, D_A:2 * D_A].set(w["rwkv_a_up"][i])
            w_lr = w_lr.at[DECAY_RANK + ICL_RANK:, 2 * D_A:].set(w["rwkv_g_up"][i])
            mix = dict(mu=row(w["rwkv_mu"][i]), w0=row(w["rwkv_w0"][i]), a0=row(w["rwkv_a0"][i]),
                       w_lr=w_lr.astype(BF16), k_k=row(w["rwkv_k_k"][i]), k_a=row(w["rwkv_k_a"][i]),
                       r_k=row(w["rwkv_r_k"][i]), ln_w=row(w["rwkv_ln_w"][i]), ln_b=row(w["rwkv_ln_b"][i]),
                       conv_w=w["conv_w"][i].astype(F32), bd=bd)
            w_in = w["ev_w_in"][i].astype(BF16)
            w_out = w["ev_w_out"][i].astype(BF16)
        else:
            a_up = jnp.zeros((LANES, C_KEY), F32).at[:GLA_GATE_RANK].set(w["gla_a_up"][i])
            mix = dict(a_up=a_up.astype(BF16), a_b=row(w["gla_a_b"][i]), gn=row(w["gla_norm"][i]))
            w_in = jnp.pad(w["od_w_in"][i], ((0, 0), (0, ODD_IN_PAD - w["od_w_in"].shape[-1]))).astype(BF16)
            w_out = w["od_w_out"][i].astype(BF16)
        layers.append(dict(ffn0=ffn(layer, 0), ffn1=ffn(layer, 1), gm=row(w["mix_norm"][layer]),
                           w_in=w_in, w_out=w_out, mix=mix))
    return layers, row(w["final_norm"])


def _trunk(x, shift0, wkv0, conv0, gla0, layers, final_norm, tm, bg, tb, c):
    bsz, t, _ = x.shape
    n = bsz * t
    x = x.reshape(n, D_MODEL)
    new_shift, new_wkv, new_conv, new_gla = [], [], [], []
    for layer, lw in enumerate(layers):
        i = layer // 2
        x, proj = _pre_call(x, lw["ffn0"], lw["gm"], lw["w_in"], tm)
        proj = proj.reshape(bsz, t, -1)
        if layer % 2 == 0:
            mix, sh, wkv, cv = _even_call(proj, shift0[i][:, None, :], wkv0[i], conv0[i], lw["mix"],
                                          bg, tb, c)
            new_shift.append(sh[:, 0, :])
            new_wkv.append(wkv)
            new_conv.append(cv)
        else:
            mix, st = _odd_call(proj, jnp.swapaxes(gla0[i], -1, -2), lw["mix"], bg, tb, c)
            new_gla.append(jnp.swapaxes(st, -1, -2))
        x = _post_call(x, mix.reshape(n, D_MODEL), lw["w_out"], lw["ffn1"], final_norm, tm,
                       final=layer == len(layers) - 1)
    return (x.reshape(bsz, t, D_MODEL), jnp.stack(new_shift), jnp.stack(new_wkv),
            jnp.stack(new_conv), jnp.stack(new_gla))


def _tiles(bsz, t):
    n = bsz * t
    tm = 512 if n % 512 == 0 else n
    bg = 2 if bsz % 2 == 0 else 1
    tb = 256 if t % 256 == 0 else t
    c = 64 if tb % 64 == 0 else tb
    return tm, bg, tb, c


def kernel(x_prompt, x_sample, state_rwkv_shift, state_rwkv_wkv, state_conv, state_gla, ffn_norm, ffn_w_gate, ffn_w_up, ffn_w_down, mix_norm, ev_w_in, ev_w_out, rwkv_mu, rwkv_w0, rwkv_w_up, rwkv_a0, rwkv_a_up, rwkv_g_up, rwkv_k_k, rwkv_k_a, rwkv_r_k, rwkv_ln_w, rwkv_ln_b, conv_w, od_w_in, od_w_out, gla_a_up, gla_a_b, gla_norm, final_norm):
    w = dict(ffn_norm=ffn_norm, ffn_w_gate=ffn_w_gate, ffn_w_up=ffn_w_up, ffn_w_down=ffn_w_down,
             mix_norm=mix_norm, ev_w_in=ev_w_in, ev_w_out=ev_w_out, rwkv_mu=rwkv_mu, rwkv_w0=rwkv_w0,
             rwkv_w_up=rwkv_w_up, rwkv_a0=rwkv_a0, rwkv_a_up=rwkv_a_up, rwkv_g_up=rwkv_g_up,
             rwkv_k_k=rwkv_k_k, rwkv_k_a=rwkv_k_a, rwkv_r_k=rwkv_r_k.reshape(rwkv_r_k.shape[0], -1),
             rwkv_ln_w=rwkv_ln_w, rwkv_ln_b=rwkv_ln_b, conv_w=conv_w, od_w_in=od_w_in,
             od_w_out=od_w_out, gla_a_up=gla_a_up, gla_a_b=gla_a_b, gla_norm=gla_norm,
             final_norm=final_norm)
    layers, fnorm = _prepare(w)
    n_even, n_odd = state_rwkv_shift.shape[0], state_gla.shape[0]
    bp = x_prompt.shape[0]
    zeros = lambda *s: jnp.zeros(s, F32)
    y_p, p_shift, p_wkv, p_conv, p_gla = _trunk(
        x_prompt, zeros(n_even, bp, A_COLS), zeros(n_even, bp, A_HEADS, A_HEAD_DIM, A_HEAD_DIM),
        zeros(n_even, bp, CONV_W - 1, D_B), zeros(n_odd, bp, C_HEADS, C_DK, C_DV),
        layers, fnorm, *_tiles(*x_prompt.shape[:2]))
    y_s, s_shift, s_wkv, s_conv, s_gla = _trunk(
        x_sample, state_rwkv_shift, state_rwkv_wkv, state_conv, state_gla,
        layers, fnorm, *_tiles(*x_sample.shape[:2]))
    return (y_p, y_s, p_shift, p_wkv, p_conv, p_gla, s_shift, s_wkv, s_conv, s_gla)
```

```python
import functools

import jax
import jax.numpy as jnp
from jax import lax
from jax.experimental import pallas as pl
from jax.experimental.pallas import tpu as pltpu

F32 = jnp.float32
BF16 = jnp.bfloat16

D_MODEL = 1024
D_FF = 2816
FFN_RES = 0.5
A_HEADS, A_HEAD_DIM = 8, 64
D_A = A_HEADS * A_HEAD_DIM
DECAY_RANK, ICL_RANK, GATE_RANK = 64, 64, 128
LOW_RANK = DECAY_RANK + ICL_RANK + GATE_RANK
A_COLS = 3 * D_A + LOW_RANK
D_B = D_MODEL - D_A
CONV_W = 3
EVEN_IN = A_COLS + 3 * D_B
C_HEADS, C_DK, C_DV = 4, 128, 256
C_KEY, C_VAL = C_HEADS * C_DK, C_HEADS * C_DV
GLA_GATE_RANK = 16
GLA_GATE_NORM = 16.0
ODD_MAIN = 2 * C_KEY + 2 * C_VAL
RMS_EPS = 1e-6
GN_EPS = 64e-5
L2_EPS = 1e-12
DECAY_SCALE = 0.6065306597126334

LANES = 128
MXU_WIDTH = 256
FF_CHUNK = MXU_WIDTH
ODD_IN_PAD = ODD_MAIN + LANES
VMEM_LIMIT = 56 * 1024 * 1024


def _dot(a, b):
    return jnp.dot(a, b, preferred_element_type=F32)


def _dot_nt(a, b):
    return lax.dot_general(a, b, (((1,), (1,)), ((), ())), preferred_element_type=F32)


def _dot_tn(a, b):
    return lax.dot_general(a, b, (((0,), (0,)), ((), ())), preferred_element_type=F32)


def _split(x, parts):
    out = []
    for _ in range(parts - 1):
        p = x.astype(BF16)
        out.append(p)
        x = x - p.astype(F32)
    out.append(x.astype(BF16))
    return out


def _cumsum_rows(tri_bf16, x):
    acc = None
    for p in _split(x, 3):
        t = _dot(tri_bf16, p)
        acc = t if acc is None else acc + t
    return acc


def _head_sums(x, bd_bf16, parts):
    w = bd_bf16.shape[0]
    pieces = _split(x, parts)
    tiles = []
    for j in range(x.shape[1] // w):
        acc = None
        for p in pieces:
            t = _dot(p[:, j * w:(j + 1) * w], bd_bf16)
            acc = t if acc is None else acc + t
        tiles.append(acc)
    return jnp.concatenate(tiles, axis=1)


def _rms(x, g):
    return x * lax.rsqrt(jnp.mean(x * x, axis=-1, keepdims=True) + RMS_EPS) * g


def _log_sigmoid(x):
    return jnp.minimum(x, 0.0) - jnp.log(1.0 + jnp.exp(-jnp.abs(x)))


def _tril_masks(c):
    row = lax.broadcasted_iota(jnp.int32, (c, c), 0)
    col = lax.broadcasted_iota(jnp.int32, (c, c), 1)
    return row > col, row >= col


def _ffn_residual(x, g_ref, wg_ref, wu_ref, wd_ref):
    h = _rms(x, g_ref[...]).astype(BF16)
    acc = None
    for j in range(D_FF // FF_CHUNK):
        cols = slice(j * FF_CHUNK, (j + 1) * FF_CHUNK)
        a = _dot(h, wg_ref[:, cols])
        b = _dot(h, wu_ref[:, cols])
        act = (a * jax.nn.sigmoid(a) * b).astype(BF16)
        d = _dot(act, wd_ref[cols, :])
        acc = d if acc is None else acc + d
    return x + FFN_RES * acc


def _pre_kernel(x_ref, g_ref, wg_ref, wu_ref, wd_ref, gm_ref, win_ref, x1_ref, proj_ref):
    x1 = _ffn_residual(x_ref[...], g_ref, wg_ref, wu_ref, wd_ref)
    x1_ref[...] = x1
    hm = _rms(x1, gm_ref[...]).astype(BF16)
    proj_ref[...] = _dot(hm, win_ref[...])


def _pre_even_kernel(x_ref, g_ref, wg_ref, wu_ref, wd_ref, gm_ref, win_ref,
                     shift0_ref, conv0_ref, mu_ref, cw_ref,
                     x1_ref, xs_ref, yb_ref, shift_ref, conv_ref, sh_s, cv_s, *, tm, t):
    @pl.when((pl.program_id(0) * tm) % t == 0)
    def _():
        sh_s[...] = shift0_ref[...]
        cv_s[...] = conv0_ref[...]

    x1 = _ffn_residual(x_ref[...], g_ref, wg_ref, wu_ref, wd_ref)
    x1_ref[...] = x1
    hm = _rms(x1, gm_ref[...]).astype(BF16)
    row = lax.broadcasted_iota(jnp.int32, (tm, 1), 0)

    pa = _dot(hm, win_ref[:, :A_COLS])
    prev = jnp.where(row == 0, sh_s[...], pltpu.roll(pa, 1, 0))
    last = pa[tm - 1:tm, :]
    sh_s[...] = last
    shift_ref[...] = last
    xs_ref[...] = pa + (prev - pa) * mu_ref[...]

    pb = _dot(hm, win_ref[:, A_COLS:])
    u = pb[:, D_B:2 * D_B] * pb[:, 2 * D_B:]
    cv = cv_s[...]
    u1 = jnp.where(row == 0, cv[1:2, :], pltpu.roll(u, 1, 0))
    u2 = jnp.where(row == 0, cv[0:1, :], jnp.where(row == 1, cv[1:2, :], pltpu.roll(u, 2, 0)))
    conv = cw_ref[0:1, :] * u2 + cw_ref[1:2, :] * u1 + cw_ref[2:3, :] * u
    hist = u[tm - 2:tm, :]
    cv_s[...] = hist
    conv_ref[...] = hist
    yb_ref[...] = (pb[:, :D_B] * conv).astype(BF16)


def _post_kernel(x_ref, *refs, n_parts, final):
    y_refs = refs[:n_parts]
    wout_ref, g_ref, wg_ref, wu_ref, wd_ref, gf_ref, out_ref = refs[n_parts:]
    mix, off = None, 0
    for y_ref in y_refs:
        part = _dot(y_ref[...], wout_ref[off:off + y_ref.shape[1], :])
        mix = part if mix is None else mix + part
        off += y_ref.shape[1]
    x3 = _ffn_residual(x_ref[...] + mix, g_ref, wg_ref, wu_ref, wd_ref)
    if final:
        x3 = _rms(x3, gf_ref[...])
    out_ref[...] = x3


def _resident(a):
    nd = a.ndim
    return pl.BlockSpec(a.shape, lambda *_: (0,) * nd, pipeline_mode=pl.Buffered(1))


def _row_tile(tm, width):
    return pl.BlockSpec((tm, width), lambda i: (i, 0))


def _pre_call(x, ffn, gm, w_in, tm):
    n = x.shape[0]
    p = w_in.shape[1]
    g, wg, wu, wd = ffn
    return pl.pallas_call(
        _pre_kernel,
        grid=(n // tm,),
        in_specs=[_row_tile(tm, D_MODEL)] + [_resident(a) for a in (g, wg, wu, wd, gm, w_in)],
        out_specs=[_row_tile(tm, D_MODEL), _row_tile(tm, p)],
        out_shape=[jax.ShapeDtypeStruct((n, D_MODEL), F32), jax.ShapeDtypeStruct((n, p), F32)],
        compiler_params=pltpu.CompilerParams(dimension_semantics=("arbitrary",),
                                             vmem_limit_bytes=VMEM_LIMIT),
        name="pre_ffn_proj",
    )(x, g, wg, wu, wd, gm, w_in)


def _pre_even_call(x, ffn, gm, w_in, shift0, conv0, mu, conv_w, tm, t):
    n = x.shape[0]
    bsz = n // t
    assert t % tm == 0
    g, wg, wu, wd = ffn
    per_seq = lambda *tail: pl.BlockSpec((None,) + tail, lambda i: ((i * tm) // t,) + (0,) * len(tail))
    state_specs = [per_seq(1, A_COLS), per_seq(CONV_W - 1, D_B)]
    return pl.pallas_call(
        functools.partial(_pre_even_kernel, tm=tm, t=t),
        grid=(n // tm,),
        in_specs=[_row_tile(tm, D_MODEL)] + [_resident(a) for a in (g, wg, wu, wd, gm, w_in)]
        + state_specs + [_resident(mu), _resident(conv_w)],
        out_specs=[_row_tile(tm, D_MODEL), _row_tile(tm, A_COLS), _row_tile(tm, D_B)] + state_specs,
        out_shape=[jax.ShapeDtypeStruct((n, D_MODEL), F32), jax.ShapeDtypeStruct((n, A_COLS), F32),
                   jax.ShapeDtypeStruct((n, D_B), BF16),
                   jax.ShapeDtypeStruct((bsz, 1, A_COLS), F32),
                   jax.ShapeDtypeStruct((bsz, CONV_W - 1, D_B), F32)],
        scratch_shapes=[pltpu.VMEM((1, A_COLS), F32), pltpu.VMEM((CONV_W - 1, D_B), F32)],
        compiler_params=pltpu.CompilerParams(dimension_semantics=("arbitrary",),
                                             vmem_limit_bytes=VMEM_LIMIT),
        name="pre_ffn_proj_shift_conv",
    )(x, g, wg, wu, wd, gm, w_in, shift0, conv0, mu, conv_w)


def _post_call(x, ys, w_out, ffn, gf, tm, final):
    n = x.shape[0]
    g, wg, wu, wd = ffn
    return pl.pallas_call(
        functools.partial(_post_kernel, n_parts=len(ys), final=final),
        grid=(n // tm,),
        in_specs=[_row_tile(tm, D_MODEL)] + [_row_tile(tm, y.shape[1]) for y in ys]
        + [_resident(a) for a in (w_out, g, wg, wu, wd, gf)],
        out_specs=_row_tile(tm, D_MODEL),
        out_shape=jax.ShapeDtypeStruct((n, D_MODEL), F32),
        compiler_params=pltpu.CompilerParams(dimension_semantics=("arbitrary",),
                                             vmem_limit_bytes=VMEM_LIMIT),
        name="post_proj_ffn",
    )(x, *ys, w_out, g, wg, wu, wd, gf)


HEAD_GROUP = MXU_WIDTH // A_HEAD_DIM


def _head_block(h):
    q, j = divmod(h, HEAD_GROUP)
    return q, slice(j * A_HEAD_DIM, (j + 1) * A_HEAD_DIM)


def _blockdiag(x, lane_head):
    zero = jnp.zeros_like(x)
    return jnp.concatenate([jnp.where(lane_head == h, x, zero) for h in range(HEAD_GROUP)], axis=0)


def _feat_head():
    return lax.broadcasted_iota(jnp.int32, (1, MXU_WIDTH), 1) // A_HEAD_DIM


def _rwkv_prepare(chunks, c):
    _, incl_cc = _tril_masks(c)
    tri = jnp.where(incl_cc, 1.0, 0.0).astype(BF16)
    hw = HEAD_GROUP * c
    row = lax.broadcasted_iota(jnp.int32, (c, hw), 0)
    col = lax.broadcasted_iota(jnp.int32, (c, hw), 1) % c
    strict, incl = row > col, row >= col
    feat_head = _feat_head()
    frame_head = lax.broadcasted_iota(jnp.int32, (1, hw), 1) // c
    units = []
    for ci, ins in enumerate(chunks):
        for i, (r, l, k, v, kk, b) in enumerate(ins):
            cum = _cumsum_rows(tri, l)
            ctot = cum[c - 1:c, :]
            e_neg = jnp.exp(-cum)
            e_end = jnp.exp(ctot - cum)
            kap = kk * jnp.exp(cum - l)
            rt = r * jnp.exp(cum)
            bt = (b * e_neg).astype(BF16)
            kt = (k * e_neg).astype(BF16)
            b2e = b * e_end
            k2e = k * e_end
            gam = jnp.exp(ctot)
            for q in range(D_A // MXU_WIDTH):
                sl = slice(q * MXU_WIDTH, (q + 1) * MXU_WIDTH)
                units.append(dict(
                    chunk=ci, seq=i, group=q, gam=gam[:, sl], v=v[:, sl],
                    lhs=jnp.concatenate([kap[:, sl], rt[:, sl]], axis=0).astype(BF16),
                    rhs=jnp.concatenate([_blockdiag(bt[:, sl], feat_head),
                                         _blockdiag(kt[:, sl], feat_head)], axis=0),
                    kall=jnp.concatenate([k2e[:, sl], b2e[:, sl]], axis=0).astype(BF16)))
    g = [_dot_nt(u["lhs"], u["rhs"]) for u in units]
    for u, gj in zip(units, g):
        a_kq = jnp.concatenate([jnp.where(strict, gj[:c, hw:], 0.0), jnp.where(incl, gj[c:, hw:], 0.0)],
                               axis=0).astype(BF16)
        u["kv"] = _dot(a_kq, _blockdiag(u["v"].astype(BF16), feat_head))
        u["q_b"] = jnp.where(incl, gj[c:, :hw], 0.0).astype(BF16)
    n = range(len(units))
    t = [jnp.where(strict, -gj[:c, :hw], 0.0) for gj in g]
    m = [_dot(x.astype(BF16), _blockdiag(x.astype(BF16), frame_head)) for x in t]
    for _ in range(c.bit_length() - 3):
        both = [_dot(jnp.concatenate([t[j], m[j]], axis=0).astype(BF16),
                     _blockdiag(m[j].astype(BF16), frame_head)) for j in n]
        t = [t[j] + m[j] + both[j][:c] for j in n]
        m = [both[j][c:] for j in n]
    for j, u in enumerate(units):
        u["t"] = (t[j] + m[j] + _dot(t[j].astype(BF16), _blockdiag(m[j].astype(BF16), frame_head))
                  ).astype(BF16)
    return [[u for u in units if u["chunk"] == ci] for ci in range(len(chunks))]


def _rwkv_advance(s_ref, units, c, n_seq):
    feat_head = _feat_head()
    state_mask = (lax.broadcasted_iota(jnp.int32, (MXU_WIDTH, MXU_WIDTH), 0) // A_HEAD_DIM
                  == lax.broadcasted_iota(jnp.int32, (MXU_WIDTH, MXU_WIDTH), 1) // A_HEAD_DIM)
    s0 = [s_ref[u["seq"], u["group"]] for u in units]
    p = [_dot_nt(u["lhs"], s.astype(BF16)) for u, s in zip(units, s0)]
    w = [pj[:c] + u["kv"][:c] for u, pj in zip(units, p)]
    uu = [wj + _dot(u["t"], _blockdiag(wj.astype(BF16), feat_head)) for u, wj in zip(units, w)]
    ys = [pj[c:] + u["kv"][c:] - _dot(u["q_b"], _blockdiag(uj.astype(BF16), feat_head))
          for u, pj, uj in zip(units, p, uu)]
    for u, s, uj in zip(units, s0, uu):
        z = jnp.concatenate([u["v"], -uj], axis=0).astype(BF16)
        s_ref[u["seq"], u["group"]] = s * u["gam"] + jnp.where(state_mask, _dot_tn(z, u["kall"]), 0.0)
    return [jnp.concatenate([y for u, y in zip(units, ys) if u["seq"] == i], axis=1) for i in range(n_seq)]


def _even_kernel(xs_ref, wkv0_ref, w0_ref, a0_ref, wlr_ref, kk_ref, ka_ref, rk_ref, lnw_ref, lnb_ref,
                 bd_ref, y_ref, wkv_ref, s_ref, *, bg, tb, c):
    t_idx = pl.program_id(1)

    @pl.when(t_idx == 0)
    def _():
        s_ref[...] = jnp.zeros_like(s_ref)
        for i in range(bg):
            for h in range(A_HEADS):
                q, d = _head_block(h)
                s_ref[i, q, d, d] = wkv0_ref[i, h]

    lane = lax.broadcasted_iota(jnp.int32, (1, LOW_RANK), 1)
    bd = bd_ref[...]
    inv_n = 1.0 / A_HEAD_DIM
    seqs = range(bg)
    stack = lambda xs: jnp.concatenate(xs, axis=0)
    chunks, keep = [], []
    for ci in range(tb // c):
        rows = slice(ci * c, (ci + 1) * c)
        xs = [xs_ref[i, rows, :] for i in seqs]
        lo = stack([x[:, 3 * D_A:] for x in xs])
        lo_act = jnp.where(lane < DECAY_RANK, jnp.tanh(lo),
                           jnp.where(lane < DECAY_RANK + ICL_RANK, lo, jax.nn.sigmoid(lo)))
        z = _dot(lo_act.astype(BF16), wlr_ref[...])
        r = stack([x[:, :D_A] for x in xs])
        k = stack([x[:, D_A:2 * D_A] for x in xs])
        v = stack([x[:, 2 * D_A:3 * D_A] for x in xs])
        log_decay = jax.nn.sigmoid(w0_ref[...] + z[:, :D_A]) * (-DECAY_SCALE)
        a = jax.nn.sigmoid(a0_ref[...] + z[:, D_A:2 * D_A])
        kk = k * kk_ref[...]
        kk = kk * jnp.minimum(lax.rsqrt(_head_sums(kk * kk, bd, 1)), 1.0 / L2_EPS)
        k_eff = k * (1.0 + (a - 1.0) * ka_ref[...])
        b = kk * a
        chunks.append([tuple(x[i * c:(i + 1) * c] for x in (r, log_decay, k_eff, v, kk, b)) for i in seqs])
        keep.append((r * k_eff * rk_ref[...], v, z[:, 2 * D_A:]))
    prepared = _rwkv_prepare(chunks, c)
    for ci, (rk, v, gate) in enumerate(keep):
        rows = slice(ci * c, (ci + 1) * c)
        y = stack(_rwkv_advance(s_ref, prepared[ci], c, bg))
        d = y - _head_sums(y, bd, 2) * inv_n
        var = _head_sums(d * d, bd, 1) * inv_n
        yn = d * lax.rsqrt(var + GN_EPS) * lnw_ref[...] + lnb_ref[...]
        bonus = _head_sums(rk, bd, 1) * v
        y_a = ((yn + bonus) * gate).astype(BF16)
        for i in seqs:
            y_ref[i, rows, :] = y_a[i * c:(i + 1) * c]

    @pl.when(t_idx == pl.num_programs(1) - 1)
    def _():
        for i in range(bg):
            for h in range(A_HEADS):
                q, d = _head_block(h)
                wkv_ref[i, h] = s_ref[i, q, d, d]


def _even_call(xs, wkv0, prm, bg, tb, c):
    bsz, t, _ = xs.shape
    params = (prm["w0"], prm["a0"], prm["w_lr"], prm["k_k"], prm["k_a"], prm["r_k"],
              prm["ln_w"], prm["ln_b"], prm["bd"])
    state_spec = pl.BlockSpec((bg, A_HEADS, A_HEAD_DIM, A_HEAD_DIM), lambda b, i: (b, 0, 0, 0))
    return pl.pallas_call(
        functools.partial(_even_kernel, bg=bg, tb=tb, c=c),
        grid=(bsz // bg, t // tb),
        in_specs=[pl.BlockSpec((bg, tb, A_COLS), lambda b, i: (b, i, 0)), state_spec]
        + [pl.BlockSpec(a.shape, lambda b, i, nd=a.ndim: (0,) * nd) for a in params],
        out_specs=[pl.BlockSpec((bg, tb, D_A), lambda b, i: (b, i, 0)), state_spec],
        out_shape=[jax.ShapeDtypeStruct((bsz, t, D_A), BF16),
                   jax.ShapeDtypeStruct((bsz, A_HEADS, A_HEAD_DIM, A_HEAD_DIM), F32)],
        scratch_shapes=[pltpu.VMEM((bg, D_A // MXU_WIDTH, MXU_WIDTH, MXU_WIDTH), F32)],
        compiler_params=pltpu.CompilerParams(dimension_semantics=("arbitrary", "arbitrary"),
                                             vmem_limit_bytes=VMEM_LIMIT),
        name="even_mixer",
    )(xs, wkv0, *params)


def _odd_kernel(proj_ref, gla0_ref, aup_ref, ab_ref, gn_ref, y_ref, gla_ref, st_ref, *, bg, tb, c):
    t_idx = pl.program_id(1)

    @pl.when(t_idx == 0)
    def _():
        st_ref[...] = gla0_ref[...]

    _, incl = _tril_masks(c)
    tri = jnp.where(incl, 1.0, 0.0).astype(BF16)
    gn = gn_ref[...]
    scale = C_DK ** -0.5
    units = []
    for i in range(bg):
        a_lo = proj_ref[i, :, ODD_MAIN:]
        la = _log_sigmoid(_dot(a_lo.astype(BF16), aup_ref[...]) + ab_ref[...]) * (1.0 / GLA_GATE_NORM)
        for ci in range(tb // c):
            rows = slice(ci * c, (ci + 1) * c)
            cum = _cumsum_rows(tri, la[rows])
            ctot = cum[c - 1:c, :]
            k = proj_ref[i, rows, C_KEY:2 * C_KEY]
            qd = (proj_ref[i, rows, :C_KEY] * scale * jnp.exp(cum)).astype(BF16)
            kd = (k * jnp.exp(-cum)).astype(BF16)
            k2e = (k * jnp.exp(ctot - cum)).astype(BF16)
            gam = jnp.exp(ctot)
            for h in range(C_HEADS):
                ks = slice(h * C_DK, (h + 1) * C_DK)
                cols = slice(2 * C_KEY + h * C_DV, 2 * C_KEY + (h + 1) * C_DV)
                units.append(dict(seq=i, chunk=ci, head=h, rows=rows, qd=qd[:, ks], kd=kd[:, ks],
                                  k2e=k2e[:, ks], gam=gam[:, ks],
                                  v=proj_ref[i, rows, cols].astype(BF16)))
    scores = [jnp.where(incl, _dot_nt(u["qd"], u["kd"]), 0.0).astype(BF16) for u in units]
    intra = [_dot(s, u["v"]) for s, u in zip(scores, units)]
    upd = [_dot_tn(u["v"], u["k2e"]) for u in units]
    state = {(i, h): st_ref[i, h] for i in range(bg) for h in range(C_HEADS)}
    inter = []
    for ci in range(tb // c):
        for u, up in zip(units, upd):
            if u["chunk"] == ci:
                key = (u["seq"], u["head"])
                inter.append((u, _dot_nt(u["qd"], state[key].astype(BF16))))
                state[key] = state[key] * u["gam"] + up
    for key, s in state.items():
        st_ref[key] = s
    inter = {(u["seq"], u["chunk"], u["head"]): x for u, x in inter}
    for u, a in zip(units, intra):
        o = a + inter[(u["seq"], u["chunk"], u["head"])]
        o = o * lax.rsqrt(jnp.mean(o * o, axis=-1, keepdims=True) + RMS_EPS) * gn
        gcols = slice(2 * C_KEY + C_VAL + u["head"] * C_DV, 2 * C_KEY + C_VAL + (u["head"] + 1) * C_DV)
        g = proj_ref[u["seq"], u["rows"], gcols]
        y_ref[u["seq"], u["rows"], u["head"] * C_DV:(u["head"] + 1) * C_DV] = (
            o * (g * jax.nn.sigmoid(g))).astype(BF16)

    @pl.when(t_idx == pl.num_programs(1) - 1)
    def _():
        gla_ref[...] = st_ref[...]


def _odd_call(proj, gla0_t, prm, bg, tb, c):
    bsz, t, _ = proj.shape
    params = (prm["a_up"], prm["a_b"], prm["gn"])
    state_spec = pl.BlockSpec((bg, C_HEADS, C_DV, C_DK), lambda b, i: (b, 0, 0, 0))
    return pl.pallas_call(
        functools.partial(_odd_kernel, bg=bg, tb=tb, c=c),
        grid=(bsz // bg, t // tb),
        in_specs=[pl.BlockSpec((bg, tb, ODD_IN_PAD), lambda b, i: (b, i, 0)), state_spec]
        + [pl.BlockSpec(a.shape, lambda b, i, nd=a.ndim: (0,) * nd) for a in params],
        out_specs=[pl.BlockSpec((bg, tb, D_MODEL), lambda b, i: (b, i, 0)), state_spec],
        out_shape=[jax.ShapeDtypeStruct((bsz, t, D_MODEL), BF16),
                   jax.ShapeDtypeStruct((bsz, C_HEADS, C_DV, C_DK), F32)],
        scratch_shapes=[pltpu.VMEM((bg, C_HEADS, C_DV, C_DK), F32)],
        compiler_params=pltpu.CompilerParams(dimension_semantics=("arbitrary", "arbitrary"),
                                             vmem_limit_bytes=VMEM_LIMIT),
        name="odd_mixer",
    )(proj, gla0_t, *params)


def _prepare(w):
    depth = w["ffn_norm"].shape[0]
    row = lambda a: a.reshape(1, -1).astype(F32)

    def ffn(layer, j):
        return (row(w["ffn_norm"][layer, j]), w["ffn_w_gate"][layer, j].astype(BF16),
                w["ffn_w_up"][layer, j].astype(BF16), w["ffn_w_down"][layer, j].astype(BF16))

    head = jnp.arange(MXU_WIDTH) // A_HEAD_DIM
    bd = (head[:, None] == head[None, :]).astype(BF16)
    layers = []
    for layer in range(depth):
        i = layer // 2
        if layer % 2 == 0:
            w_lr = jnp.zeros((LOW_RANK, 3 * D_A), F32)
            w_lr = w_lr.at[:DECAY_RANK, :D_A].set(w["rwkv_w_up"][i])
            w_lr = w_lr.at[DECAY_RANK:DECAY_RANK + ICL_RANK, D_A:2 * D_A].set(w["rwkv_a_up"][i])
            w_lr = w_lr.at[DECAY_RANK + ICL_RANK:, 2 * D_A:].set(w["rwkv_g_up"][i])
            mix = dict(mu=row(w["rwkv_mu"][i]), w0=row(w["rwkv_w0"][i]), a0=row(w["rwkv_a0"][i]),
                       w_lr=w_lr.astype(BF16), k_k=row(w["rwkv_k_k"][i]), k_a=row(w["rwkv_k_a"][i]),
                       r_k=row(w["rwkv_r_k"][i]), ln_w=row(w["rwkv_ln_w"][i]), ln_b=row(w["rwkv_ln_b"][i]),
                       conv_w=w["conv_w"][i].astype(F32), bd=bd)
            w_in = w["ev_w_in"][i].astype(BF16)
            w_out = w["ev_w_out"][i].astype(BF16)
        else:
            a_up = jnp.zeros((LANES, C_KEY), F32).at[:GLA_GATE_RANK].set(w["gla_a_up"][i])
            mix = dict(a_up=a_up.astype(BF16), a_b=row(w["gla_a_b"][i]), gn=row(w["gla_norm"][i]))
            w_in = jnp.pad(w["od_w_in"][i], ((0, 0), (0, ODD_IN_PAD - w["od_w_in"].shape[-1]))).astype(BF16)
            w_out = w["od_w_out"][i].astype(BF16)
        layers.append(dict(ffn0=ffn(layer, 0), ffn1=ffn(layer, 1), gm=row(w["mix_norm"][layer]),
                           w_in=w_in, w_out=w_out, mix=mix))
    return layers, row(w["final_norm"])


def _trunk(x, shift0, wkv0, conv0, gla0, layers, final_norm, tm, bg, tb, c):
    bsz, t, _ = x.shape
    n = bsz * t
    x = x.reshape(n, D_MODEL)
    new_shift, new_wkv, new_conv, new_gla = [], [], [], []
    for layer, lw in enumerate(layers):
        i = layer // 2
        if layer % 2 == 0:
            x, xs, y_b, sh, cv = _pre_even_call(x, lw["ffn0"], lw["gm"], lw["w_in"], shift0[i][:, None, :],
                                                conv0[i], lw["mix"]["mu"], lw["mix"]["conv_w"],
                                                min(tm, t), t)
            y_a, wkv = _even_call(xs.reshape(bsz, t, A_COLS), wkv0[i], lw["mix"], bg, tb, c)
            ys = [y_a.reshape(n, D_A), y_b]
            new_shift.append(sh[:, 0, :])
            new_wkv.append(wkv)
            new_conv.append(cv)
        else:
            x, proj = _pre_call(x, lw["ffn0"], lw["gm"], lw["w_in"], tm)
            mix, st = _odd_call(proj.reshape(bsz, t, -1), jnp.swapaxes(gla0[i], -1, -2), lw["mix"],
                                bg, tb, c)
            ys = [mix.reshape(n, D_MODEL)]
            new_gla.append(jnp.swapaxes(st, -1, -2))
        x = _post_call(x, ys, lw["w_out"], lw["ffn1"], final_norm, tm, final=layer == len(layers) - 1)
    return (x.reshape(bsz, t, D_MODEL), jnp.stack(new_shift), jnp.stack(new_wkv),
            jnp.stack(new_conv), jnp.stack(new_gla))


def _tiles(bsz, t):
    n = bsz * t
    tm = 512 if t % 512 == 0 else n
    bg = 2 if bsz % 2 == 0 else 1
    tb = 256 if t % 256 == 0 else t
    c = 64 if tb % 64 == 0 else tb
    return tm, bg, tb, c


def kernel(x_prompt, x_sample, state_rwkv_shift, state_rwkv_wkv, state_conv, state_gla, ffn_norm, ffn_w_gate, ffn_w_up, ffn_w_down, mix_norm, ev_w_in, ev_w_out, rwkv_mu, rwkv_w0, rwkv_w_up, rwkv_a0, rwkv_a_up, rwkv_g_up, rwkv_k_k, rwkv_k_a, rwkv_r_k, rwkv_ln_w, rwkv_ln_b, conv_w, od_w_in, od_w_out, gla_a_up, gla_a_b, gla_norm, final_norm):
    w = dict(ffn_norm=ffn_norm, ffn_w_gate=ffn_w_gate, ffn_w_up=ffn_w_up, ffn_w_down=ffn_w_down,
             mix_norm=mix_norm, ev_w_in=ev_w_in, ev_w_out=ev_w_out, rwkv_mu=rwkv_mu, rwkv_w0=rwkv_w0,
             rwkv_w_up=rwkv_w_up, rwkv_a0=rwkv_a0, rwkv_a_up=rwkv_a_up, rwkv_g_up=rwkv_g_up,
             rwkv_k_k=rwkv_k_k, rwkv_k_a=rwkv_k_a, rwkv_r_k=rwkv_r_k.reshape(rwkv_r_k.shape[0], -1),
             rwkv_ln_w=rwkv_ln_w, rwkv_ln_b=rwkv_ln_b, conv_w=conv_w, od_w_in=od_w_in,
             od_w_out=od_w_out, gla_a_up=gla_a_up, gla_a_b=gla_a_b, gla_norm=gla_norm,
             final_norm=final_norm)
    layers, fnorm = _prepare(w)
    n_even, n_odd = state_rwkv_shift.shape[0], state_gla.shape[0]
    bp = x_prompt.shape[0]
    zeros = lambda *s: jnp.zeros(s, F32)
    y_p, p_shift, p_wkv, p_conv, p_gla = _trunk(
        x_prompt, zeros(n_even, bp, A_COLS), zeros(n_even, bp, A_HEADS, A_HEAD_DIM, A_HEAD_DIM),
        zeros(n_even, bp, CONV_W - 1, D_B), zeros(n_odd, bp, C_HEADS, C_DK, C_DV),
        layers, fnorm, *_tiles(*x_prompt.shape[:2]))
    y_s, s_shift, s_wkv, s_conv, s_gla = _trunk(
        x_sample, state_rwkv_shift, state_rwkv_wkv, state_conv, state_gla,
        layers, fnorm, *_tiles(*x_sample.shape[:2]))
    return (y_p, y_s, p_shift, p_wkv, p_conv, p_gla, s_shift, s_wkv, s_conv, s_gla)
```

```python
import functools

import jax
import jax.numpy as jnp
from jax import lax
from jax.experimental import pallas as pl
from jax.experimental.pallas import tpu as pltpu

F32 = jnp.float32
BF16 = jnp.bfloat16

D_MODEL = 1024
D_FF = 2816
FFN_RES = 0.5
A_HEADS, A_HEAD_DIM = 8, 64
D_A = A_HEADS * A_HEAD_DIM
DECAY_RANK, ICL_RANK, GATE_RANK = 64, 64, 128
LOW_RANK = DECAY_RANK + ICL_RANK + GATE_RANK
A_COLS = 3 * D_A + LOW_RANK
D_B = D_MODEL - D_A
CONV_W = 3
EVEN_IN = A_COLS + 3 * D_B
C_HEADS, C_DK, C_DV = 4, 128, 256
C_KEY, C_VAL = C_HEADS * C_DK, C_HEADS * C_DV
GLA_GATE_RANK = 16
GLA_GATE_NORM = 16.0
ODD_MAIN = 2 * C_KEY + 2 * C_VAL
RMS_EPS = 1e-6
GN_EPS = 64e-5
L2_EPS = 1e-12
DECAY_SCALE = 0.6065306597126334

LANES = 128
MXU_WIDTH = 256
FF_CHUNK = MXU_WIDTH
ODD_IN_PAD = ODD_MAIN + LANES
VMEM_LIMIT = 56 * 1024 * 1024


def _dot(a, b):
    return jnp.dot(a, b, preferred_element_type=F32)


def _dot_nt(a, b):
    return lax.dot_general(a, b, (((1,), (1,)), ((), ())), preferred_element_type=F32)


def _dot_tn(a, b):
    return lax.dot_general(a, b, (((0,), (0,)), ((), ())), preferred_element_type=F32)


def _split(x, parts):
    out = []
    for _ in range(parts - 1):
        p = x.astype(BF16)
        out.append(p)
        x = x - p.astype(F32)
    out.append(x.astype(BF16))
    return out


def _cumsum_rows(tri_bf16, x):
    acc = None
    for p in _split(x, 2):
        t = _dot(tri_bf16, p)
        acc = t if acc is None else acc + t
    return acc


def _head_sums(x, bd_bf16, parts):
    w = bd_bf16.shape[0]
    pieces = _split(x, parts)
    tiles = []
    for j in range(x.shape[1] // w):
        acc = None
        for p in pieces:
            t = _dot(p[:, j * w:(j + 1) * w], bd_bf16)
            acc = t if acc is None else acc + t
        tiles.append(acc)
    return jnp.concatenate(tiles, axis=1)


def _rms(x, g):
    return x * lax.rsqrt(jnp.mean(x * x, axis=-1, keepdims=True) + RMS_EPS) * g


def _log_sigmoid(x):
    return jnp.minimum(x, 0.0) - jnp.log(1.0 + jnp.exp(-jnp.abs(x)))


def _tril_masks(c):
    row = lax.broadcasted_iota(jnp.int32, (c, c), 0)
    col = lax.broadcasted_iota(jnp.int32, (c, c), 1)
    return row > col, row >= col


def _ffn_residual(x, g_ref, wg_ref, wu_ref, wd_ref):
    h = _rms(x, g_ref[...]).astype(BF16)
    acc = None
    for j in range(D_FF // FF_CHUNK):
        cols = slice(j * FF_CHUNK, (j + 1) * FF_CHUNK)
        a = _dot(h, wg_ref[:, cols])
        b = _dot(h, wu_ref[:, cols])
        act = (a * jax.nn.sigmoid(a) * b).astype(BF16)
        d = _dot(act, wd_ref[cols, :])
        acc = d if acc is None else acc + d
    return x + FFN_RES * acc


def _pre_kernel(x_ref, g_ref, wg_ref, wu_ref, wd_ref, gm_ref, win_ref, x1_ref, proj_ref):
    x1 = _ffn_residual(x_ref[...], g_ref, wg_ref, wu_ref, wd_ref)
    x1_ref[...] = x1
    hm = _rms(x1, gm_ref[...]).astype(BF16)
    proj_ref[...] = _dot(hm, win_ref[...])


def _pre_even_kernel(x_ref, g_ref, wg_ref, wu_ref, wd_ref, gm_ref, win_ref,
                     shift0_ref, conv0_ref, mu_ref, cw_ref,
                     x1_ref, xs_ref, yb_ref, shift_ref, conv_ref, sh_s, cv_s, *, tm, t):
    per_tile = tm // t
    if per_tile <= 1:
        @pl.when((pl.program_id(0) * tm) % t == 0)
        def _():
            sh_s[...] = shift0_ref[...]
            cv_s[...] = conv0_ref[...]
        starts = [(0, sh_s[...], cv_s[...])]
    else:
        starts = [(s * t, shift0_ref[s], conv0_ref[s]) for s in range(per_tile)]

    x1 = _ffn_residual(x_ref[...], g_ref, wg_ref, wu_ref, wd_ref)
    x1_ref[...] = x1
    hm = _rms(x1, gm_ref[...]).astype(BF16)
    row = lax.broadcasted_iota(jnp.int32, (tm, 1), 0)

    pa = _dot(hm, win_ref[:, :A_COLS])
    prev = pltpu.roll(pa, 1, 0)
    for r0, sh, _ in starts:
        prev = jnp.where(row == r0, sh, prev)
    xs_ref[...] = pa + (prev - pa) * mu_ref[...]

    pb = _dot(hm, win_ref[:, A_COLS:])
    u = pb[:, D_B:2 * D_B] * pb[:, 2 * D_B:]
    u1 = pltpu.roll(u, 1, 0)
    u2 = pltpu.roll(u, 2, 0)
    for r0, _, cv in starts:
        u1 = jnp.where(row == r0, cv[1:2, :], u1)
        u2 = jnp.where(row == r0, cv[0:1, :], jnp.where(row == r0 + 1, cv[1:2, :], u2))
    conv = cw_ref[0:1, :] * u2 + cw_ref[1:2, :] * u1 + cw_ref[2:3, :] * u
    yb_ref[...] = (pb[:, :D_B] * conv).astype(BF16)

    if per_tile <= 1:
        sh_s[...] = shift_ref[...] = pa[tm - 1:tm, :]
        cv_s[...] = conv_ref[...] = u[tm - 2:tm, :]
    else:
        for s in range(per_tile):
            shift_ref[s] = pa[(s + 1) * t - 1:(s + 1) * t, :]
            conv_ref[s] = u[(s + 1) * t - 2:(s + 1) * t, :]


def _post_kernel(x_ref, *refs, n_parts, final):
    y_refs = refs[:n_parts]
    wout_ref, g_ref, wg_ref, wu_ref, wd_ref, gf_ref, out_ref = refs[n_parts:]
    mix, off = None, 0
    for y_ref in y_refs:
        part = _dot(y_ref[...], wout_ref[off:off + y_ref.shape[1], :])
        mix = part if mix is None else mix + part
        off += y_ref.shape[1]
    x3 = _ffn_residual(x_ref[...] + mix, g_ref, wg_ref, wu_ref, wd_ref)
    if final:
        x3 = _rms(x3, gf_ref[...])
    out_ref[...] = x3


def _resident(w):
    a, idx = w if isinstance(w, tuple) else (w, ())
    tail = a.shape[len(idx):]
    spec = pl.BlockSpec((None,) * len(idx) + tail, lambda *_: idx + (0,) * len(tail),
                        pipeline_mode=pl.Buffered(1))
    return spec, a


def _row_tile(tm, width):
    return pl.BlockSpec((tm, width), lambda i: (i, 0))


def _pre_call(x, ffn, gm, w_in, tm):
    n = x.shape[0]
    p = w_in[0].shape[-1]
    specs, weights = zip(*[_resident(a) for a in (*ffn, gm, w_in)])
    return pl.pallas_call(
        _pre_kernel,
        grid=(n // tm,),
        in_specs=[_row_tile(tm, D_MODEL), *specs],
        out_specs=[_row_tile(tm, D_MODEL), _row_tile(tm, p)],
        out_shape=[jax.ShapeDtypeStruct((n, D_MODEL), F32), jax.ShapeDtypeStruct((n, p), F32)],
        compiler_params=pltpu.CompilerParams(dimension_semantics=("arbitrary",),
                                             vmem_limit_bytes=VMEM_LIMIT),
        name="pre_ffn_proj",
    )(x, *weights)


def _pre_even_call(x, ffn, gm, w_in, shift0, conv0, mu, conv_w, tm, t):
    n = x.shape[0]
    bsz = n // t
    assert t % tm == 0 or tm % t == 0
    if tm <= t:
        per_seq = lambda *tail: pl.BlockSpec((None,) + tail, lambda i: ((i * tm) // t,) + (0,) * len(tail))
    else:
        per_seq = lambda *tail: pl.BlockSpec((tm // t,) + tail, lambda i: (i,) + (0,) * len(tail))
    state_specs = [per_seq(1, A_COLS), per_seq(CONV_W - 1, D_B)]
    specs, weights = zip(*[_resident(a) for a in (*ffn, gm, w_in)])
    tail_specs, tail_weights = zip(*[_resident(a) for a in (mu, conv_w)])
    return pl.pallas_call(
        functools.partial(_pre_even_kernel, tm=tm, t=t),
        grid=(n // tm,),
        in_specs=[_row_tile(tm, D_MODEL), *specs, *state_specs, *tail_specs],
        out_specs=[_row_tile(tm, D_MODEL), _row_tile(tm, A_COLS), _row_tile(tm, D_B)] + state_specs,
        out_shape=[jax.ShapeDtypeStruct((n, D_MODEL), F32), jax.ShapeDtypeStruct((n, A_COLS), F32),
                   jax.ShapeDtypeStruct((n, D_B), BF16),
                   jax.ShapeDtypeStruct((bsz, 1, A_COLS), F32),
                   jax.ShapeDtypeStruct((bsz, CONV_W - 1, D_B), F32)],
        scratch_shapes=[pltpu.VMEM((1, A_COLS), F32), pltpu.VMEM((CONV_W - 1, D_B), F32)],
        compiler_params=pltpu.CompilerParams(dimension_semantics=("arbitrary",),
                                             vmem_limit_bytes=VMEM_LIMIT),
        name="pre_ffn_proj_shift_conv",
    )(x, *weights, shift0, conv0, *tail_weights)


def _post_call(x, ys, w_out, ffn, gf, tm, final):
    n = x.shape[0]
    specs, weights = zip(*[_resident(a) for a in (w_out, *ffn, gf)])
    return pl.pallas_call(
        functools.partial(_post_kernel, n_parts=len(ys), final=final),
        grid=(n // tm,),
        in_specs=[_row_tile(tm, D_MODEL)] + [_row_tile(tm, y.shape[1]) for y in ys] + list(specs),
        out_specs=_row_tile(tm, D_MODEL),
        out_shape=jax.ShapeDtypeStruct((n, D_MODEL), F32),
        compiler_params=pltpu.CompilerParams(dimension_semantics=("arbitrary",),
                                             vmem_limit_bytes=VMEM_LIMIT),
        name="post_proj_ffn",
    )(x, *ys, *weights)


HEAD_GROUP = MXU_WIDTH // A_HEAD_DIM


def _head_block(h):
    q, j = divmod(h, HEAD_GROUP)
    return q, slice(j * A_HEAD_DIM, (j + 1) * A_HEAD_DIM)


def _blockdiag(x, lane_head):
    zero = jnp.zeros_like(x)
    return jnp.concatenate([jnp.where(lane_head == h, x, zero) for h in range(HEAD_GROUP)], axis=0)


def _feat_head():
    return lax.broadcasted_iota(jnp.int32, (1, MXU_WIDTH), 1) // A_HEAD_DIM


def _rwkv_prepare(chunks, c):
    _, incl_cc = _tril_masks(c)
    tri = jnp.where(incl_cc, 1.0, 0.0).astype(BF16)
    hw = HEAD_GROUP * c
    row = lax.broadcasted_iota(jnp.int32, (c, hw), 0)
    col = lax.broadcasted_iota(jnp.int32, (c, hw), 1) % c
    strict, incl = row > col, row >= col
    feat_head = _feat_head()
    frame_head = lax.broadcasted_iota(jnp.int32, (1, hw), 1) // c
    units = []
    for ci, ins in enumerate(chunks):
        for i, (r, l, k, v, kk, b) in enumerate(ins):
            cum = _cumsum_rows(tri, l)
            ctot = cum[c - 1:c, :]
            e_neg = jnp.exp(-cum)
            e_end = jnp.exp(ctot - cum)
            kap = kk * jnp.exp(cum - l)
            rt = r * jnp.exp(cum)
            bt = (b * e_neg).astype(BF16)
            kt = (k * e_neg).astype(BF16)
            b2e = b * e_end
            k2e = k * e_end
            gam = jnp.exp(ctot)
            for q in range(D_A // MXU_WIDTH):
                sl = slice(q * MXU_WIDTH, (q + 1) * MXU_WIDTH)
                units.append(dict(
                    chunk=ci, seq=i, group=q, gam=gam[:, sl], v=v[:, sl],
                    lhs=jnp.concatenate([kap[:, sl], rt[:, sl]], axis=0).astype(BF16),
                    rhs=jnp.concatenate([_blockdiag(bt[:, sl], feat_head),
                                         _blockdiag(kt[:, sl], feat_head)], axis=0),
                    kall=jnp.concatenate([k2e[:, sl], b2e[:, sl]], axis=0).astype(BF16)))
    g = [_dot_nt(u["lhs"], u["rhs"]) for u in units]
    for u, gj in zip(units, g):
        a_kq = jnp.concatenate([jnp.where(strict, gj[:c, hw:], 0.0), jnp.where(incl, gj[c:, hw:], 0.0)],
                               axis=0).astype(BF16)
        u["kv"] = _dot(a_kq, _blockdiag(u["v"].astype(BF16), feat_head))
        u["q_b"] = jnp.where(incl, gj[c:, :hw], 0.0).astype(BF16)
    n = range(len(units))
    t = [jnp.where(strict, -gj[:c, :hw], 0.0) for gj in g]
    m = [_dot(x.astype(BF16), _blockdiag(x.astype(BF16), frame_head)) for x in t]
    for _ in range(c.bit_length() - 3):
        both = [_dot(jnp.concatenate([t[j], m[j]], axis=0).astype(BF16),
                     _blockdiag(m[j].astype(BF16), frame_head)) for j in n]
        t = [t[j] + m[j] + both[j][:c] for j in n]
        m = [both[j][c:] for j in n]
    for j, u in enumerate(units):
        u["t"] = (t[j] + m[j] + _dot(t[j].astype(BF16), _blockdiag(m[j].astype(BF16), frame_head))
                  ).astype(BF16)
    return [[u for u in units if u["chunk"] == ci] for ci in range(len(chunks))]


def _rwkv_advance(s_ref, units, c, n_seq):
    feat_head = _feat_head()
    state_mask = (lax.broadcasted_iota(jnp.int32, (MXU_WIDTH, MXU_WIDTH), 0) // A_HEAD_DIM
                  == lax.broadcasted_iota(jnp.int32, (MXU_WIDTH, MXU_WIDTH), 1) // A_HEAD_DIM)
    s0 = [s_ref[u["seq"], u["group"]] for u in units]
    p = [_dot_nt(u["lhs"], s.astype(BF16)) for u, s in zip(units, s0)]
    w = [pj[:c] + u["kv"][:c] for u, pj in zip(units, p)]
    uu = [wj + _dot(u["t"], _blockdiag(wj.astype(BF16), feat_head)) for u, wj in zip(units, w)]
    ys = [pj[c:] + u["kv"][c:] - _dot(u["q_b"], _blockdiag(uj.astype(BF16), feat_head))
          for u, pj, uj in zip(units, p, uu)]
    for u, s, uj in zip(units, s0, uu):
        z = jnp.concatenate([u["v"], -uj], axis=0).astype(BF16)
        s_ref[u["seq"], u["group"]] = s * u["gam"] + jnp.where(state_mask, _dot_tn(z, u["kall"]), 0.0)
    return [jnp.concatenate([y for u, y in zip(units, ys) if u["seq"] == i], axis=1) for i in range(n_seq)]


def _even_kernel(xs_ref, wkv0_ref, w0_ref, a0_ref, wlr_ref, kk_ref, ka_ref, rk_ref, lnw_ref, lnb_ref,
                 bd_ref, y_ref, wkv_ref, s_ref, *, bg, tb, c):
    t_idx = pl.program_id(1)

    @pl.when(t_idx == 0)
    def _():
        s_ref[...] = jnp.zeros_like(s_ref)
        for i in range(bg):
            for h in range(A_HEADS):
                q, d = _head_block(h)
                s_ref[i, q, d, d] = wkv0_ref[i, h]

    lane = lax.broadcasted_iota(jnp.int32, (1, LOW_RANK), 1)
    bd = bd_ref[...]
    inv_n = 1.0 / A_HEAD_DIM
    seqs = range(bg)
    stack = lambda xs: jnp.concatenate(xs, axis=0)
    chunks, keep = [], []
    for ci in range(tb // c):
        rows = slice(ci * c, (ci + 1) * c)
        xs = [xs_ref[i, rows, :] for i in seqs]
        lo = stack([x[:, 3 * D_A:] for x in xs])
        lo_act = jnp.where(lane < DECAY_RANK, jnp.tanh(lo),
                           jnp.where(lane < DECAY_RANK + ICL_RANK, lo, jax.nn.sigmoid(lo)))
        z = _dot(lo_act.astype(BF16), wlr_ref[...])
        r = stack([x[:, :D_A] for x in xs])
        k = stack([x[:, D_A:2 * D_A] for x in xs])
        v = stack([x[:, 2 * D_A:3 * D_A] for x in xs])
        log_decay = jax.nn.sigmoid(w0_ref[...] + z[:, :D_A]) * (-DECAY_SCALE)
        a = jax.nn.sigmoid(a0_ref[...] + z[:, D_A:2 * D_A])
        kk = k * kk_ref[...]
        kk = kk * jnp.minimum(lax.rsqrt(_head_sums(kk * kk, bd, 1)), 1.0 / L2_EPS)
        k_eff = k * (1.0 + (a - 1.0) * ka_ref[...])
        b = kk * a
        chunks.append([tuple(x[i * c:(i + 1) * c] for x in (r, log_decay, k_eff, v, kk, b)) for i in seqs])
        keep.append((r * k_eff * rk_ref[...], v, z[:, 2 * D_A:]))
    prepared = _rwkv_prepare(chunks, c)
    for ci, (rk, v, gate) in enumerate(keep):
        rows = slice(ci * c, (ci + 1) * c)
        y = stack(_rwkv_advance(s_ref, prepared[ci], c, bg))
        d = y - _head_sums(y, bd, 1) * inv_n
        var = _head_sums(d * d, bd, 1) * inv_n
        yn = d * lax.rsqrt(var + GN_EPS) * lnw_ref[...] + lnb_ref[...]
        bonus = _head_sums(rk, bd, 1) * v
        y_a = ((yn + bonus) * gate).astype(BF16)
        for i in seqs:
            y_ref[i, rows, :] = y_a[i * c:(i + 1) * c]

    @pl.when(t_idx == pl.num_programs(1) - 1)
    def _():
        for i in range(bg):
            for h in range(A_HEADS):
                q, d = _head_block(h)
                wkv_ref[i, h] = s_ref[i, q, d, d]


def _even_call(xs, wkv0, prm, bg, tb, c):
    bsz, t, _ = xs.shape
    params = (prm["w0"], prm["a0"], prm["w_lr"], prm["k_k"], prm["k_a"], prm["r_k"],
              prm["ln_w"], prm["ln_b"], prm["bd"])
    state_spec = pl.BlockSpec((bg, A_HEADS, A_HEAD_DIM, A_HEAD_DIM), lambda b, i: (b, 0, 0, 0))
    return pl.pallas_call(
        functools.partial(_even_kernel, bg=bg, tb=tb, c=c),
        grid=(bsz // bg, t // tb),
        in_specs=[pl.BlockSpec((bg, tb, A_COLS), lambda b, i: (b, i, 0)), state_spec]
        + [pl.BlockSpec(a.shape, lambda b, i, nd=a.ndim: (0,) * nd) for a in params],
        out_specs=[pl.BlockSpec((bg, tb, D_A), lambda b, i: (b, i, 0)), state_spec],
        out_shape=[jax.ShapeDtypeStruct((bsz, t, D_A), BF16),
                   jax.ShapeDtypeStruct((bsz, A_HEADS, A_HEAD_DIM, A_HEAD_DIM), F32)],
        scratch_shapes=[pltpu.VMEM((bg, D_A // MXU_WIDTH, MXU_WIDTH, MXU_WIDTH), F32)],
        compiler_params=pltpu.CompilerParams(dimension_semantics=("arbitrary", "arbitrary"),
                                             vmem_limit_bytes=VMEM_LIMIT),
        name="even_mixer",
    )(xs, wkv0, *params)


def _odd_kernel(proj_ref, gla0_ref, aup_ref, ab_ref, gn_ref, y_ref, gla_ref, st_ref, *, bg, tb, c):
    t_idx = pl.program_id(1)

    @pl.when(t_idx == 0)
    def _():
        st_ref[...] = gla0_ref[...]

    _, incl = _tril_masks(c)
    tri = jnp.where(incl, 1.0, 0.0).astype(BF16)
    gn = gn_ref[...]
    scale = C_DK ** -0.5
    units = []
    for i in range(bg):
        a_lo = proj_ref[i, :, ODD_MAIN:]
        la = _log_sigmoid(_dot(a_lo.astype(BF16), aup_ref[...]) + ab_ref[...]) * (1.0 / GLA_GATE_NORM)
        for ci in range(tb // c):
            rows = slice(ci * c, (ci + 1) * c)
            cum = _cumsum_rows(tri, la[rows])
            ctot = cum[c - 1:c, :]
            k = proj_ref[i, rows, C_KEY:2 * C_KEY]
            qd = (proj_ref[i, rows, :C_KEY] * scale * jnp.exp(cum)).astype(BF16)
            kd = (k * jnp.exp(-cum)).astype(BF16)
            k2e = (k * jnp.exp(ctot - cum)).astype(BF16)
            gam = jnp.exp(ctot)
            for h in range(C_HEADS):
                ks = slice(h * C_DK, (h + 1) * C_DK)
                cols = slice(2 * C_KEY + h * C_DV, 2 * C_KEY + (h + 1) * C_DV)
                units.append(dict(seq=i, chunk=ci, head=h, rows=rows, qd=qd[:, ks], kd=kd[:, ks],
                                  k2e=k2e[:, ks], gam=gam[:, ks],
                                  v=proj_ref[i, rows, cols].astype(BF16)))
    scores = [jnp.where(incl, _dot_nt(u["qd"], u["kd"]), 0.0).astype(BF16) for u in units]
    intra = [_dot(s, u["v"]) for s, u in zip(scores, units)]
    upd = [_dot_tn(u["v"], u["k2e"]) for u in units]
    state = {(i, h): st_ref[i, h] for i in range(bg) for h in range(C_HEADS)}
    inter = []
    for ci in range(tb // c):
        for u, up in zip(units, upd):
            if u["chunk"] == ci:
                key = (u["seq"], u["head"])
                inter.append((u, _dot_nt(u["qd"], state[key].astype(BF16))))
                state[key] = state[key] * u["gam"] + up
    for key, s in state.items():
        st_ref[key] = s
    inter = {(u["seq"], u["chunk"], u["head"]): x for u, x in inter}
    for u, a in zip(units, intra):
        o = a + inter[(u["seq"], u["chunk"], u["head"])]
        o = o * lax.rsqrt(jnp.mean(o * o, axis=-1, keepdims=True) + RMS_EPS) * gn
        gcols = slice(2 * C_KEY + C_VAL + u["head"] * C_DV, 2 * C_KEY + C_VAL + (u["head"] + 1) * C_DV)
        g = proj_ref[u["seq"], u["rows"], gcols]
        y_ref[u["seq"], u["rows"], u["head"] * C_DV:(u["head"] + 1) * C_DV] = (
            o * (g * jax.nn.sigmoid(g))).astype(BF16)

    @pl.when(t_idx == pl.num_programs(1) - 1)
    def _():
        gla_ref[...] = st_ref[...]


def _odd_call(proj, gla0_t, prm, bg, tb, c):
    bsz, t, _ = proj.shape
    params = (prm["a_up"], prm["a_b"], prm["gn"])
    state_spec = pl.BlockSpec((bg, C_HEADS, C_DV, C_DK), lambda b, i: (b, 0, 0, 0))
    return pl.pallas_call(
        functools.partial(_odd_kernel, bg=bg, tb=tb, c=c),
        grid=(bsz // bg, t // tb),
        in_specs=[pl.BlockSpec((bg, tb, ODD_IN_PAD), lambda b, i: (b, i, 0)), state_spec]
        + [pl.BlockSpec(a.shape, lambda b, i, nd=a.ndim: (0,) * nd) for a in params],
        out_specs=[pl.BlockSpec((bg, tb, D_MODEL), lambda b, i: (b, i, 0)), state_spec],
        out_shape=[jax.ShapeDtypeStruct((bsz, t, D_MODEL), BF16),
                   jax.ShapeDtypeStruct((bsz, C_HEADS, C_DV, C_DK), F32)],
        scratch_shapes=[pltpu.VMEM((bg, C_HEADS, C_DV, C_DK), F32)],
        compiler_params=pltpu.CompilerParams(dimension_semantics=("arbitrary", "arbitrary"),
                                             vmem_limit_bytes=VMEM_LIMIT),
        name="odd_mixer",
    )(proj, gla0_t, *params)


def _prepare(w):
    depth = w["ffn_norm"].shape[0]
    row = lambda a: a.reshape(1, -1).astype(F32)

    wg, wu, wd = (w[k].astype(BF16) for k in ("ffn_w_gate", "ffn_w_up", "ffn_w_down"))
    ev_in, ev_out, od_out = (w[k].astype(BF16) for k in ("ev_w_in", "ev_w_out", "od_w_out"))
    od_in = jnp.pad(w["od_w_in"], ((0, 0), (0, 0), (0, ODD_IN_PAD - w["od_w_in"].shape[-1]))).astype(BF16)

    def ffn(layer, j):
        return (row(w["ffn_norm"][layer, j]), (wg, (layer, j)), (wu, (layer, j)), (wd, (layer, j)))

    head = jnp.arange(MXU_WIDTH) // A_HEAD_DIM
    bd = (head[:, None] == head[None, :]).astype(BF16)
    layers = []
    for layer in range(depth):
        i = layer // 2
        if layer % 2 == 0:
            w_lr = jnp.zeros((LOW_RANK, 3 * D_A), F32)
            w_lr = w_lr.at[:DECAY_RANK, :D_A].set(w["rwkv_w_up"][i])
            w_lr = w_lr.at[DECAY_RANK:DECAY_RANK + ICL_RANK, D_A:2 * D_A].set(w["rwkv_a_up"][i])
            w_lr = w_lr.at[DECAY_RANK + ICL_RANK:, 2 * D_A:].set(w["rwkv_g_up"][i])
            mix = dict(mu=row(w["rwkv_mu"][i]), w0=row(w["rwkv_w0"][i]), a0=row(w["rwkv_a0"][i]),
                       w_lr=w_lr.astype(BF16), k_k=row(w["rwkv_k_k"][i]), k_a=row(w["rwkv_k_a"][i]),
                       r_k=row(w["rwkv_r_k"][i]), ln_w=row(w["rwkv_ln_w"][i]), ln_b=row(w["rwkv_ln_b"][i]),
                       conv_w=w["conv_w"][i].astype(F32), bd=bd)
            w_in, w_out = (ev_in, (i,)), (ev_out, (i,))
        else:
            a_up = jnp.zeros((LANES, C_KEY), F32).at[:GLA_GATE_RANK].set(w["gla_a_up"][i])
            mix = dict(a_up=a_up.astype(BF16), a_b=row(w["gla_a_b"][i]), gn=row(w["gla_norm"][i]))
            w_in, w_out = (od_in, (i,)), (od_out, (i,))
        layers.append(dict(ffn0=ffn(layer, 0), ffn1=ffn(layer, 1), gm=row(w["mix_norm"][layer]),
                           w_in=w_in, w_out=w_out, mix=mix))
    return layers, row(w["final_norm"])


def _trunk(x, shift0, wkv0, conv0, gla0, layers, final_norm, tm, bg, tb, c):
    bsz, t, _ = x.shape
    n = bsz * t
    x = x.reshape(n, D_MODEL)
    new_shift, new_wkv, new_conv, new_gla = [], [], [], []
    for layer, lw in enumerate(layers):
        i = layer // 2
        if layer % 2 == 0:
            x, xs, y_b, sh, cv = _pre_even_call(x, lw["ffn0"], lw["gm"], lw["w_in"], shift0[i][:, None, :],
                                                conv0[i], lw["mix"]["mu"], lw["mix"]["conv_w"], tm, t)
            y_a, wkv = _even_call(xs.reshape(bsz, t, A_COLS), wkv0[i], lw["mix"], bg, tb, c)
            ys = [y_a.reshape(n, D_A), y_b]
            new_shift.append(sh[:, 0, :])
            new_wkv.append(wkv)
            new_conv.append(cv)
        else:
            x, proj = _pre_call(x, lw["ffn0"], lw["gm"], lw["w_in"], tm)
            mix, st = _odd_call(proj.reshape(bsz, t, -1), jnp.swapaxes(gla0[i], -1, -2), lw["mix"],
                                bg, tb, c)
            ys = [mix.reshape(n, D_MODEL)]
            new_gla.append(jnp.swapaxes(st, -1, -2))
        tm_post = 2 * tm if tm >= 512 and n % (2 * tm) == 0 else tm
        x = _post_call(x, ys, lw["w_out"], lw["ffn1"], final_norm, tm_post,
                       final=layer == len(layers) - 1)
    return (x.reshape(bsz, t, D_MODEL), jnp.stack(new_shift), jnp.stack(new_wkv),
            jnp.stack(new_conv), jnp.stack(new_gla))


def _tiles(bsz, t):
    n = bsz * t
    tm = 512 if t % 512 == 0 else n
    bg = 2 if bsz % 2 == 0 else 1
    tb = 256 if t % 256 == 0 else t
    c = 64 if tb % 64 == 0 else tb
    return tm, bg, tb, c


def kernel(x_prompt, x_sample, state_rwkv_shift, state_rwkv_wkv, state_conv, state_gla, ffn_norm, ffn_w_gate, ffn_w_up, ffn_w_down, mix_norm, ev_w_in, ev_w_out, rwkv_mu, rwkv_w0, rwkv_w_up, rwkv_a0, rwkv_a_up, rwkv_g_up, rwkv_k_k, rwkv_k_a, rwkv_r_k, rwkv_ln_w, rwkv_ln_b, conv_w, od_w_in, od_w_out, gla_a_up, gla_a_b, gla_norm, final_norm):
    w = dict(ffn_norm=ffn_norm, ffn_w_gate=ffn_w_gate, ffn_w_up=ffn_w_up, ffn_w_down=ffn_w_down,
             mix_norm=mix_norm, ev_w_in=ev_w_in, ev_w_out=ev_w_out, rwkv_mu=rwkv_mu, rwkv_w0=rwkv_w0,
             rwkv_w_up=rwkv_w_up, rwkv_a0=rwkv_a0, rwkv_a_up=rwkv_a_up, rwkv_g_up=rwkv_g_up,
             rwkv_k_k=rwkv_k_k, rwkv_k_a=rwkv_k_a, rwkv_r_k=rwkv_r_k.reshape(rwkv_r_k.shape[0], -1),
             rwkv_ln_w=rwkv_ln_w, rwkv_ln_b=rwkv_ln_b, conv_w=conv_w, od_w_in=od_w_in,
             od_w_out=od_w_out, gla_a_up=gla_a_up, gla_a_b=gla_a_b, gla_norm=gla_norm,
             final_norm=final_norm)
    layers, fnorm = _prepare(w)
    n_even, n_odd = state_rwkv_shift.shape[0], state_gla.shape[0]
    bp = x_prompt.shape[0]
    zeros = lambda *s: jnp.zeros(s, F32)
    y_p, p_shift, p_wkv, p_conv, p_gla = _trunk(
        x_prompt, zeros(n_even, bp, A_COLS), zeros(n_even, bp, A_HEADS, A_HEAD_DIM, A_HEAD_DIM),
        zeros(n_even, bp, CONV_W - 1, D_B), zeros(n_odd, bp, C_HEADS, C_DK, C_DV),
        layers, fnorm, *_tiles(*x_prompt.shape[:2]))
    y_s, s_shift, s_wkv, s_conv, s_gla = _trunk(
        x_sample, state_rwkv_shift, state_rwkv_wkv, state_conv, state_gla,
        layers, fnorm, *_tiles(*x_sample.shape[:2]))
    return (y_p, y_s, p_shift, p_wkv, p_conv, p_gla, s_shift, s_wkv, s_conv, s_gla)
```

```python
import functools

import jax
import jax.numpy as jnp
from jax import lax
from jax.experimental import pallas as pl
from jax.experimental.pallas import tpu as pltpu

F32 = jnp.float32
BF16 = jnp.bfloat16

D_MODEL = 1024
D_FF = 2816
FFN_RES = 0.5
A_HEADS, A_HEAD_DIM = 8, 64
D_A = A_HEADS * A_HEAD_DIM
DECAY_RANK, ICL_RANK, GATE_RANK = 64, 64, 128
LOW_RANK = DECAY_RANK + ICL_RANK + GATE_RANK
A_COLS = 3 * D_A + LOW_RANK
D_B = D_MODEL - D_A
CONV_W = 3
EVEN_IN = A_COLS + 3 * D_B
C_HEADS, C_DK, C_DV = 4, 128, 256
C_KEY, C_VAL = C_HEADS * C_DK, C_HEADS * C_DV
GLA_GATE_RANK = 16
GLA_GATE_NORM = 16.0
ODD_MAIN = 2 * C_KEY + 2 * C_VAL
RMS_EPS = 1e-6
GN_EPS = 64e-5
L2_EPS = 1e-12
DECAY_SCALE = 0.6065306597126334

LANES = 128
MXU_WIDTH = 256
FF_CHUNK = MXU_WIDTH
POST_HALF_MIN = 512
ODD_IN_PAD = ODD_MAIN + LANES
VMEM_LIMIT = 56 * 1024 * 1024


def _dot(a, b):
    return jnp.dot(a, b, preferred_element_type=F32)


def _dot_nt(a, b):
    return lax.dot_general(a, b, (((1,), (1,)), ((), ())), preferred_element_type=F32)


def _dot_tn(a, b):
    return lax.dot_general(a, b, (((0,), (0,)), ((), ())), preferred_element_type=F32)


def _split(x, parts):
    out = []
    for _ in range(parts - 1):
        p = x.astype(BF16)
        out.append(p)
        x = x - p.astype(F32)
    out.append(x.astype(BF16))
    return out


def _cumsum_rows(tri_bf16, x):
    acc = None
    for p in _split(x, 2):
        t = _dot(tri_bf16, p)
        acc = t if acc is None else acc + t
    return acc


def _head_sums(x, bd_bf16, parts):
    w = bd_bf16.shape[0]
    pieces = _split(x, parts)
    tiles = []
    for j in range(x.shape[1] // w):
        acc = None
        for p in pieces:
            t = _dot(p[:, j * w:(j + 1) * w], bd_bf16)
            acc = t if acc is None else acc + t
        tiles.append(acc)
    return jnp.concatenate(tiles, axis=1)


def _rms(x, g):
    return x * lax.rsqrt(jnp.mean(x * x, axis=-1, keepdims=True) + RMS_EPS) * g


def _log_sigmoid(x):
    return jnp.minimum(x, 0.0) - jnp.log(1.0 + jnp.exp(-jnp.abs(x)))


def _tril_masks(c):
    row = lax.broadcasted_iota(jnp.int32, (c, c), 0)
    col = lax.broadcasted_iota(jnp.int32, (c, c), 1)
    return row > col, row >= col


def _ffn_residual(x, g_ref, wg_ref, wu_ref, wd_ref):
    h = _rms(x, g_ref[...]).astype(BF16)
    acts = []
    for start in range(0, D_FF, FF_CHUNK):
        cols = slice(start, min(start + FF_CHUNK, D_FF))
        a = _dot(h, wg_ref[:, cols])
        b = _dot(h, wu_ref[:, cols])
        acts.append((a * jax.nn.sigmoid(a) * b).astype(BF16))
    return x + FFN_RES * _dot(jnp.concatenate(acts, axis=1), wd_ref[...])


def _pre_kernel(x_ref, g_ref, wg_ref, wu_ref, wd_ref, gm_ref, win_ref, x1_ref, proj_ref):
    x1 = _ffn_residual(x_ref[...], g_ref, wg_ref, wu_ref, wd_ref)
    x1_ref[...] = x1
    tm = x1.shape[0]
    for rows in ([slice(0, tm // 2), slice(tm // 2, tm)] if tm % 16 == 0 else [slice(0, tm)]):
        hm = _rms(x1[rows], gm_ref[...]).astype(BF16)
        proj_ref[rows, :] = _dot(hm, win_ref[...])


def _pre_even_kernel(x_ref, g_ref, wg_ref, wu_ref, wd_ref, gm_ref, win_ref,
                     shift0_ref, conv0_ref, mu_ref, cw_ref,
                     x1_ref, xs_ref, yb_ref, shift_ref, conv_ref, sh_s, cv_s, *, tm, t):
    per_tile = tm // t
    if per_tile <= 1:
        @pl.when((pl.program_id(0) * tm) % t == 0)
        def _():
            sh_s[...] = shift0_ref[...]
            cv_s[...] = conv0_ref[...]
        starts = [(0, sh_s[...], cv_s[...])]
    else:
        starts = [(s * t, shift0_ref[s], conv0_ref[s]) for s in range(per_tile)]

    x1 = _ffn_residual(x_ref[...], g_ref, wg_ref, wu_ref, wd_ref)
    x1_ref[...] = x1
    halves = [slice(0, tm // 2), slice(tm // 2, tm)] if tm % 16 == 0 else [slice(0, tm)]
    hm = [_rms(x1[rows], gm_ref[...]).astype(BF16) for rows in halves]
    row = lax.broadcasted_iota(jnp.int32, (tm, 1), 0)

    pa = jnp.concatenate([_dot(h, win_ref[:, :A_COLS]) for h in hm], axis=0)
    prev = pltpu.roll(pa, 1, 0)
    for r0, sh, _ in starts:
        prev = jnp.where(row == r0, sh, prev)
    xs_ref[...] = pa + (prev - pa) * mu_ref[...]

    pb = jnp.concatenate([_dot(h, win_ref[:, A_COLS:]) for h in hm], axis=0)
    u = pb[:, D_B:2 * D_B] * pb[:, 2 * D_B:]
    u1 = pltpu.roll(u, 1, 0)
    u2 = pltpu.roll(u, 2, 0)
    for r0, _, cv in starts:
        u1 = jnp.where(row == r0, cv[1:2, :], u1)
        u2 = jnp.where(row == r0, cv[0:1, :], jnp.where(row == r0 + 1, cv[1:2, :], u2))
    conv = cw_ref[0:1, :] * u2 + cw_ref[1:2, :] * u1 + cw_ref[2:3, :] * u
    yb_ref[...] = (pb[:, :D_B] * conv).astype(BF16)

    if per_tile <= 1:
        sh_s[...] = shift_ref[...] = pa[tm - 1:tm, :]
        cv_s[...] = conv_ref[...] = u[tm - 2:tm, :]
    else:
        for s in range(per_tile):
            shift_ref[s] = pa[(s + 1) * t - 1:(s + 1) * t, :]
            conv_ref[s] = u[(s + 1) * t - 2:(s + 1) * t, :]


def _post_kernel(x_ref, *refs, n_parts, final):
    y_refs = refs[:n_parts]
    wout_ref, g_ref, wg_ref, wu_ref, wd_ref, gf_ref, out_ref = refs[n_parts:]
    tm = x_ref.shape[0]
    halves = [slice(0, tm // 2), slice(tm // 2, tm)] if tm >= 2 * POST_HALF_MIN else [slice(0, tm)]
    x2 = []
    for rows in halves:
        mix, off = None, 0
        for y_ref in y_refs:
            part = _dot(y_ref[rows, :], wout_ref[off:off + y_ref.shape[1], :])
            mix = part if mix is None else mix + part
            off += y_ref.shape[1]
        x2.append(x_ref[rows, :] + mix)
    for rows, v in zip(halves, x2):
        x3 = _ffn_residual(v, g_ref, wg_ref, wu_ref, wd_ref)
        if final:
            x3 = _rms(x3, gf_ref[...])
        out_ref[rows, :] = x3


def _resident(w):
    a, idx = w if isinstance(w, tuple) else (w, ())
    tail = a.shape[len(idx):]
    spec = pl.BlockSpec((None,) * len(idx) + tail, lambda *_: idx + (0,) * len(tail),
                        pipeline_mode=pl.Buffered(1))
    return spec, a


def _row_tile(tm, width):
    return pl.BlockSpec((tm, width), lambda i: (i, 0))


def _pre_call(x, ffn, gm, w_in, tm):
    n = x.shape[0]
    p = w_in[0].shape[-1]
    specs, weights = zip(*[_resident(a) for a in (*ffn, gm, w_in)])
    return pl.pallas_call(
        _pre_kernel,
        grid=(n // tm,),
        in_specs=[_row_tile(tm, D_MODEL), *specs],
        out_specs=[_row_tile(tm, D_MODEL), _row_tile(tm, p)],
        out_shape=[jax.ShapeDtypeStruct((n, D_MODEL), F32), jax.ShapeDtypeStruct((n, p), F32)],
        compiler_params=pltpu.CompilerParams(dimension_semantics=("arbitrary",),
                                             vmem_limit_bytes=VMEM_LIMIT),
        name="pre_ffn_proj",
    )(x, *weights)


def _pre_even_call(x, ffn, gm, w_in, shift0, conv0, mu, conv_w, tm, t):
    n = x.shape[0]
    bsz = n // t
    assert t % tm == 0 or tm % t == 0
    if tm <= t:
        per_seq = lambda *tail: pl.BlockSpec((None,) + tail, lambda i: ((i * tm) // t,) + (0,) * len(tail))
    else:
        per_seq = lambda *tail: pl.BlockSpec((tm // t,) + tail, lambda i: (i,) + (0,) * len(tail))
    state_specs = [per_seq(1, A_COLS), per_seq(CONV_W - 1, D_B)]
    specs, weights = zip(*[_resident(a) for a in (*ffn, gm, w_in)])
    tail_specs, tail_weights = zip(*[_resident(a) for a in (mu, conv_w)])
    return pl.pallas_call(
        functools.partial(_pre_even_kernel, tm=tm, t=t),
        grid=(n // tm,),
        in_specs=[_row_tile(tm, D_MODEL), *specs, *state_specs, *tail_specs],
        out_specs=[_row_tile(tm, D_MODEL), _row_tile(tm, A_COLS), _row_tile(tm, D_B)] + state_specs,
        out_shape=[jax.ShapeDtypeStruct((n, D_MODEL), F32), jax.ShapeDtypeStruct((n, A_COLS), F32),
                   jax.ShapeDtypeStruct((n, D_B), BF16),
                   jax.ShapeDtypeStruct((bsz, 1, A_COLS), F32),
                   jax.ShapeDtypeStruct((bsz, CONV_W - 1, D_B), F32)],
        scratch_shapes=[pltpu.VMEM((1, A_COLS), F32), pltpu.VMEM((CONV_W - 1, D_B), F32)],
        compiler_params=pltpu.CompilerParams(dimension_semantics=("arbitrary",),
                                             vmem_limit_bytes=VMEM_LIMIT),
        name="pre_ffn_proj_shift_conv",
    )(x, *weights, shift0, conv0, *tail_weights)


def _post_call(x, ys, w_out, ffn, gf, tm, final):
    n = x.shape[0]
    specs, weights = zip(*[_resident(a) for a in (w_out, *ffn, gf)])
    return pl.pallas_call(
        functools.partial(_post_kernel, n_parts=len(ys), final=final),
        grid=(n // tm,),
        in_specs=[_row_tile(tm, D_MODEL)] + [_row_tile(tm, y.shape[1]) for y in ys] + list(specs),
        out_specs=_row_tile(tm, D_MODEL),
        out_shape=jax.ShapeDtypeStruct((n, D_MODEL), F32),
        compiler_params=pltpu.CompilerParams(dimension_semantics=("arbitrary",),
                                             vmem_limit_bytes=VMEM_LIMIT),
        name="post_proj_ffn",
    )(x, *ys, *weights)


HEAD_GROUP = MXU_WIDTH // A_HEAD_DIM


def _head_block(h):
    q, j = divmod(h, HEAD_GROUP)
    return q, slice(j * A_HEAD_DIM, (j + 1) * A_HEAD_DIM)


def _blockdiag(x, lane_head):
    zero = jnp.zeros_like(x)
    return jnp.concatenate([jnp.where(lane_head == h, x, zero) for h in range(HEAD_GROUP)], axis=0)


def _feat_head():
    return lax.broadcasted_iota(jnp.int32, (1, MXU_WIDTH), 1) // A_HEAD_DIM


def _rwkv_prepare(chunks, c):
    _, incl_cc = _tril_masks(c)
    tri = jnp.where(incl_cc, 1.0, 0.0).astype(BF16)
    hw = HEAD_GROUP * c
    row = lax.broadcasted_iota(jnp.int32, (c, hw), 0)
    col = lax.broadcasted_iota(jnp.int32, (c, hw), 1) % c
    strict, incl = row > col, row >= col
    feat_head = _feat_head()
    frame_head = lax.broadcasted_iota(jnp.int32, (1, hw), 1) // c
    units = []
    for ci, ins in enumerate(chunks):
        for i, (r, l, k, v, kk, b) in enumerate(ins):
            cum = _cumsum_rows(tri, l)
            ctot = cum[c - 1:c, :]
            e_neg = jnp.exp(-cum)
            e_end = jnp.exp(ctot - cum)
            kap = kk * jnp.exp(cum - l)
            rt = r * jnp.exp(cum)
            bt = (b * e_neg).astype(BF16)
            kt = (k * e_neg).astype(BF16)
            b2e = b * e_end
            k2e = k * e_end
            gam = jnp.exp(ctot)
            for q in range(D_A // MXU_WIDTH):
                sl = slice(q * MXU_WIDTH, (q + 1) * MXU_WIDTH)
                units.append(dict(
                    chunk=ci, seq=i, group=q, gam=gam[:, sl], v=v[:, sl],
                    lhs=jnp.concatenate([kap[:, sl], rt[:, sl]], axis=0).astype(BF16),
                    rhs=jnp.concatenate([_blockdiag(bt[:, sl], feat_head),
                                         _blockdiag(kt[:, sl], feat_head)], axis=0),
                    kall=jnp.concatenate([k2e[:, sl], b2e[:, sl]], axis=0).astype(BF16)))
    g = [_dot_nt(u["lhs"], u["rhs"]) for u in units]
    for u, gj in zip(units, g):
        a_kq = jnp.concatenate([jnp.where(strict, gj[:c, hw:], 0.0), jnp.where(incl, gj[c:, hw:], 0.0)],
                               axis=0).astype(BF16)
        u["kv"] = _dot(a_kq, _blockdiag(u["v"].astype(BF16), feat_head))
        u["q_b"] = jnp.where(incl, gj[c:, :hw], 0.0).astype(BF16)
    n = range(len(units))
    t = [jnp.where(strict, -gj[:c, :hw], 0.0) for gj in g]
    m = [_dot(x.astype(BF16), _blockdiag(x.astype(BF16), frame_head)) for x in t]
    for _ in range(c.bit_length() - 3):
        both = [_dot(jnp.concatenate([t[j], m[j]], axis=0).astype(BF16),
                     _blockdiag(m[j].astype(BF16), frame_head)) for j in n]
        t = [t[j] + m[j] + both[j][:c] for j in n]
        m = [both[j][c:] for j in n]
    t = [(t[j] + m[j] + _dot(t[j].astype(BF16), _blockdiag(m[j].astype(BF16), frame_head))).astype(BF16)
         for j in n]
    for u, tj in zip(units, t):
        kap_b, kv_top = u["lhs"][:c], u["kv"][:c]
        khat = kap_b.astype(F32) + _dot(tj, _blockdiag(kap_b, feat_head))
        u["lhs"] = jnp.concatenate([khat.astype(BF16), u["lhs"][c:]], axis=0)
        u["tkv"] = kv_top + _dot(tj, _blockdiag(kv_top.astype(BF16), feat_head))
    return [[u for u in units if u["chunk"] == ci] for ci in range(len(chunks))]


def _rwkv_advance(s_ref, units, c, n_seq):
    feat_head = _feat_head()
    state_mask = (lax.broadcasted_iota(jnp.int32, (MXU_WIDTH, MXU_WIDTH), 0) // A_HEAD_DIM
                  == lax.broadcasted_iota(jnp.int32, (MXU_WIDTH, MXU_WIDTH), 1) // A_HEAD_DIM)
    s0 = [s_ref[u["seq"], u["group"]] for u in units]
    p = [_dot_nt(u["lhs"], s.astype(BF16)) for u, s in zip(units, s0)]
    uu = [pj[:c] + u["tkv"] for u, pj in zip(units, p)]
    ys = [pj[c:] + u["kv"][c:] - _dot(u["q_b"], _blockdiag(uj.astype(BF16), feat_head))
          for u, pj, uj in zip(units, p, uu)]
    for u, s, uj in zip(units, s0, uu):
        z = jnp.concatenate([u["v"], -uj], axis=0).astype(BF16)
        s_ref[u["seq"], u["group"]] = s * u["gam"] + jnp.where(state_mask, _dot_tn(z, u["kall"]), 0.0)
    return [jnp.concatenate([y for u, y in zip(units, ys) if u["seq"] == i], axis=1) for i in range(n_seq)]


def _even_kernel(xs_ref, wkv0_ref, w0_ref, a0_ref, wlr_ref, kk_ref, ka_ref, rk_ref, lnw_ref, lnb_ref,
                 bd_ref, y_ref, wkv_ref, s_ref, *, bg, tb, c):
    t_idx = pl.program_id(1)

    @pl.when(t_idx == 0)
    def _():
        s_ref[...] = jnp.zeros_like(s_ref)
        for i in range(bg):
            for h in range(A_HEADS):
                q, d = _head_block(h)
                s_ref[i, q, d, d] = wkv0_ref[i, h]

    lane = lax.broadcasted_iota(jnp.int32, (1, LOW_RANK), 1)
    bd = bd_ref[...]
    inv_n = 1.0 / A_HEAD_DIM
    seqs = range(bg)
    stack = lambda xs: jnp.concatenate(xs, axis=0)
    chunks, keep = [], []
    for ci in range(tb // c):
        rows = slice(ci * c, (ci + 1) * c)
        xs = [xs_ref[i, rows, :] for i in seqs]
        lo = stack([x[:, 3 * D_A:] for x in xs])
        lo_act = jnp.where(lane < DECAY_RANK, jnp.tanh(lo),
                           jnp.where(lane < DECAY_RANK + ICL_RANK, lo, jax.nn.sigmoid(lo)))
        z = _dot(lo_act.astype(BF16), wlr_ref[...])
        r = stack([x[:, :D_A] for x in xs])
        k = stack([x[:, D_A:2 * D_A] for x in xs])
        v = stack([x[:, 2 * D_A:3 * D_A] for x in xs])
        log_decay = jax.nn.sigmoid(w0_ref[...] + z[:, :D_A]) * (-DECAY_SCALE)
        a = jax.nn.sigmoid(a0_ref[...] + z[:, D_A:2 * D_A])
        kk = k * kk_ref[...]
        kk = kk * jnp.minimum(lax.rsqrt(_head_sums(kk * kk, bd, 1)), 1.0 / L2_EPS)
        k_eff = k * (1.0 + (a - 1.0) * ka_ref[...])
        b = kk * a
        chunks.append([tuple(x[i * c:(i + 1) * c] for x in (r, log_decay, k_eff, v, kk, b)) for i in seqs])
        keep.append((r * k_eff * rk_ref[...], v, z[:, 2 * D_A:]))
    prepared = _rwkv_prepare(chunks, c)
    for ci, (rk, v, gate) in enumerate(keep):
        rows = slice(ci * c, (ci + 1) * c)
        y = stack(_rwkv_advance(s_ref, prepared[ci], c, bg))
        d = y - _head_sums(y, bd, 1) * inv_n
        var = _head_sums(d * d, bd, 1) * inv_n
        yn = d * lax.rsqrt(var + GN_EPS) * lnw_ref[...] + lnb_ref[...]
        bonus = _head_sums(rk, bd, 1) * v
        y_a = ((yn + bonus) * gate).astype(BF16)
        for i in seqs:
            y_ref[i, rows, :] = y_a[i * c:(i + 1) * c]

    @pl.when(t_idx == pl.num_programs(1) - 1)
    def _():
        for i in range(bg):
            for h in range(A_HEADS):
                q, d = _head_block(h)
                wkv_ref[i, h] = s_ref[i, q, d, d]


def _even_call(xs, wkv0, prm, bg, tb, c):
    bsz, t, _ = xs.shape
    params = (prm["w0"], prm["a0"], prm["w_lr"], prm["k_k"], prm["k_a"], prm["r_k"],
              prm["ln_w"], prm["ln_b"], prm["bd"])
    state_spec = pl.BlockSpec((bg, A_HEADS, A_HEAD_DIM, A_HEAD_DIM), lambda b, i: (b, 0, 0, 0))
    return pl.pallas_call(
        functools.partial(_even_kernel, bg=bg, tb=tb, c=c),
        grid=(bsz // bg, t // tb),
        in_specs=[pl.BlockSpec((bg, tb, A_COLS), lambda b, i: (b, i, 0)), state_spec]
        + [pl.BlockSpec(a.shape, lambda b, i, nd=a.ndim: (0,) * nd) for a in params],
        out_specs=[pl.BlockSpec((bg, tb, D_A), lambda b, i: (b, i, 0)), state_spec],
        out_shape=[jax.ShapeDtypeStruct((bsz, t, D_A), BF16),
                   jax.ShapeDtypeStruct((bsz, A_HEADS, A_HEAD_DIM, A_HEAD_DIM), F32)],
        scratch_shapes=[pltpu.VMEM((bg, D_A // MXU_WIDTH, MXU_WIDTH, MXU_WIDTH), F32)],
        compiler_params=pltpu.CompilerParams(dimension_semantics=("arbitrary", "arbitrary"),
                                             vmem_limit_bytes=VMEM_LIMIT),
        name="even_mixer",
    )(xs, wkv0, *params)


def _odd_kernel(proj_ref, gla0_ref, aup_ref, ab_ref, gn_ref, y_ref, gla_ref, st_ref, *, bg, tb, c):
    t_idx = pl.program_id(1)

    @pl.when(t_idx == 0)
    def _():
        st_ref[...] = gla0_ref[...]

    _, incl = _tril_masks(c)
    tri = jnp.where(incl, 1.0, 0.0).astype(BF16)
    gn = gn_ref[...]
    scale = C_DK ** -0.5
    units = []
    for i in range(bg):
        a_lo = proj_ref[i, :, ODD_MAIN:]
        la = _log_sigmoid(_dot(a_lo.astype(BF16), aup_ref[...]) + ab_ref[...]) * (1.0 / GLA_GATE_NORM)
        for ci in range(tb // c):
            rows = slice(ci * c, (ci + 1) * c)
            cum = _cumsum_rows(tri, la[rows])
            ctot = cum[c - 1:c, :]
            k = proj_ref[i, rows, C_KEY:2 * C_KEY]
            qd = (proj_ref[i, rows, :C_KEY] * scale * jnp.exp(cum)).astype(BF16)
            kd = (k * jnp.exp(-cum)).astype(BF16)
            k2e = (k * jnp.exp(ctot - cum)).astype(BF16)
            gam = jnp.exp(ctot)
            for h in range(C_HEADS):
                ks = slice(h * C_DK, (h + 1) * C_DK)
                cols = slice(2 * C_KEY + h * C_DV, 2 * C_KEY + (h + 1) * C_DV)
                units.append(dict(seq=i, chunk=ci, head=h, rows=rows, qd=qd[:, ks], kd=kd[:, ks],
                                  k2e=k2e[:, ks], gam=gam[:, ks],
                                  v=proj_ref[i, rows, cols].astype(BF16)))
    scores = [jnp.where(incl, _dot_nt(u["qd"], u["kd"]), 0.0).astype(BF16) for u in units]
    intra = [_dot(s, u["v"]) for s, u in zip(scores, units)]
    upd = [_dot_tn(u["v"], u["k2e"]) for u in units]
    state = {(i, h): st_ref[i, h] for i in range(bg) for h in range(C_HEADS)}
    inter = []
    for ci in range(tb // c):
        for u, up in zip(units, upd):
            if u["chunk"] == ci:
                key = (u["seq"], u["head"])
                inter.append((u, _dot_nt(u["qd"], state[key].astype(BF16))))
                state[key] = state[key] * u["gam"] + up
    for key, s in state.items():
        st_ref[key] = s
    inter = {(u["seq"], u["chunk"], u["head"]): x for u, x in inter}
    for u, a in zip(units, intra):
        o = a + inter[(u["seq"], u["chunk"], u["head"])]
        o = o * lax.rsqrt(jnp.mean(o * o, axis=-1, keepdims=True) + RMS_EPS) * gn
        gcols = slice(2 * C_KEY + C_VAL + u["head"] * C_DV, 2 * C_KEY + C_VAL + (u["head"] + 1) * C_DV)
        g = proj_ref[u["seq"], u["rows"], gcols]
        y_ref[u["seq"], u["rows"], u["head"] * C_DV:(u["head"] + 1) * C_DV] = (
            o * (g * jax.nn.sigmoid(g))).astype(BF16)

    @pl.when(t_idx == pl.num_programs(1) - 1)
    def _():
        gla_ref[...] = st_ref[...]


def _odd_call(proj, gla0_t, prm, bg, tb, c):
    bsz, t, _ = proj.shape
    params = (prm["a_up"], prm["a_b"], prm["gn"])
    state_spec = pl.BlockSpec((bg, C_HEADS, C_DV, C_DK), lambda b, i: (b, 0, 0, 0))
    return pl.pallas_call(
        functools.partial(_odd_kernel, bg=bg, tb=tb, c=c),
        grid=(bsz // bg, t // tb),
        in_specs=[pl.BlockSpec((bg, tb, ODD_IN_PAD), lambda b, i: (b, i, 0)), state_spec]
        + [pl.BlockSpec(a.shape, lambda b, i, nd=a.ndim: (0,) * nd) for a in params],
        out_specs=[pl.BlockSpec((bg, tb, D_MODEL), lambda b, i: (b, i, 0)), state_spec],
        out_shape=[jax.ShapeDtypeStruct((bsz, t, D_MODEL), BF16),
                   jax.ShapeDtypeStruct((bsz, C_HEADS, C_DV, C_DK), F32)],
        scratch_shapes=[pltpu.VMEM((bg, C_HEADS, C_DV, C_DK), F32)],
        compiler_params=pltpu.CompilerParams(dimension_semantics=("arbitrary", "arbitrary"),
                                             vmem_limit_bytes=VMEM_LIMIT),
        name="odd_mixer",
    )(proj, gla0_t, *params)


def _prepare(w):
    depth = w["ffn_norm"].shape[0]
    row = lambda a: a.reshape(1, -1).astype(F32)

    wg, wu, wd = (w[k].astype(BF16) for k in ("ffn_w_gate", "ffn_w_up", "ffn_w_down"))
    ev_in, ev_out, od_out = (w[k].astype(BF16) for k in ("ev_w_in", "ev_w_out", "od_w_out"))
    od_in = jnp.pad(w["od_w_in"], ((0, 0), (0, 0), (0, ODD_IN_PAD - w["od_w_in"].shape[-1]))).astype(BF16)

    def ffn(layer, j):
        return (row(w["ffn_norm"][layer, j]), (wg, (layer, j)), (wu, (layer, j)), (wd, (layer, j)))

    head = jnp.arange(MXU_WIDTH) // A_HEAD_DIM
    bd = (head[:, None] == head[None, :]).astype(BF16)
    layers = []
    for layer in range(depth):
        i = layer // 2
        if layer % 2 == 0:
            w_lr = jnp.zeros((LOW_RANK, 3 * D_A), F32)
            w_lr = w_lr.at[:DECAY_RANK, :D_A].set(w["rwkv_w_up"][i])
            w_lr = w_lr.at[DECAY_RANK:DECAY_RANK + ICL_RANK, D_A:2 * D_A].set(w["rwkv_a_up"][i])
            w_lr = w_lr.at[DECAY_RANK + ICL_RANK:, 2 * D_A:].set(w["rwkv_g_up"][i])
            mix = dict(mu=row(w["rwkv_mu"][i]), w0=row(w["rwkv_w0"][i]), a0=row(w["rwkv_a0"][i]),
                       w_lr=w_lr.astype(BF16), k_k=row(w["rwkv_k_k"][i]), k_a=row(w["rwkv_k_a"][i]),
                       r_k=row(w["rwkv_r_k"][i]), ln_w=row(w["rwkv_ln_w"][i]), ln_b=row(w["rwkv_ln_b"][i]),
                       conv_w=w["conv_w"][i].astype(F32), bd=bd)
            w_in, w_out = (ev_in, (i,)), (ev_out, (i,))
        else:
            a_up = jnp.zeros((LANES, C_KEY), F32).at[:GLA_GATE_RANK].set(w["gla_a_up"][i])
            mix = dict(a_up=a_up.astype(BF16), a_b=row(w["gla_a_b"][i]), gn=row(w["gla_norm"][i]))
            w_in, w_out = (od_in, (i,)), (od_out, (i,))
        layers.append(dict(ffn0=ffn(layer, 0), ffn1=ffn(layer, 1), gm=row(w["mix_norm"][layer]),
                           w_in=w_in, w_out=w_out, mix=mix))
    return layers, row(w["final_norm"])


def _trunk(x, shift0, wkv0, conv0, gla0, layers, final_norm, tm, bg, tb, c):
    bsz, t, _ = x.shape
    n = bsz * t
    x = x.reshape(n, D_MODEL)
    new_shift, new_wkv, new_conv, new_gla = [], [], [], []
    for layer, lw in enumerate(layers):
        i = layer // 2
        if layer % 2 == 0:
            x, xs, y_b, sh, cv = _pre_even_call(x, lw["ffn0"], lw["gm"], lw["w_in"], shift0[i][:, None, :],
                                                conv0[i], lw["mix"]["mu"], lw["mix"]["conv_w"], tm, t)
            y_a, wkv = _even_call(xs.reshape(bsz, t, A_COLS), wkv0[i], lw["mix"], bg, tb, c)
            ys = [y_a.reshape(n, D_A), y_b]
            new_shift.append(sh[:, 0, :])
            new_wkv.append(wkv)
            new_conv.append(cv)
        else:
            x, proj = _pre_call(x, lw["ffn0"], lw["gm"], lw["w_in"], tm)
            mix, st = _odd_call(proj.reshape(bsz, t, -1), jnp.swapaxes(gla0[i], -1, -2), lw["mix"],
                                bg, tb, c)
            ys = [mix.reshape(n, D_MODEL)]
            new_gla.append(jnp.swapaxes(st, -1, -2))
        tm_post = 2 * tm if tm >= 512 and n % (2 * tm) == 0 else tm
        x = _post_call(x, ys, lw["w_out"], lw["ffn1"], final_norm, tm_post,
                       final=layer == len(layers) - 1)
    return (x.reshape(bsz, t, D_MODEL), jnp.stack(new_shift), jnp.stack(new_wkv),
            jnp.stack(new_conv), jnp.stack(new_gla))


def _tiles(bsz, t):
    n = bsz * t
    tm = 512 if t % 512 == 0 else n
    bg = 2 if bsz % 2 == 0 else 1
    tb = 256 if t % 256 == 0 else t
    c = 64 if tb % 64 == 0 else tb
    return tm, bg, tb, c


def kernel(x_prompt, x_sample, state_rwkv_shift, state_rwkv_wkv, state_conv, state_gla, ffn_norm, ffn_w_gate, ffn_w_up, ffn_w_down, mix_norm, ev_w_in, ev_w_out, rwkv_mu, rwkv_w0, rwkv_w_up, rwkv_a0, rwkv_a_up, rwkv_g_up, rwkv_k_k, rwkv_k_a, rwkv_r_k, rwkv_ln_w, rwkv_ln_b, conv_w, od_w_in, od_w_out, gla_a_up, gla_a_b, gla_norm, final_norm):
    w = dict(ffn_norm=ffn_norm, ffn_w_gate=ffn_w_gate, ffn_w_up=ffn_w_up, ffn_w_down=ffn_w_down,
             mix_norm=mix_norm, ev_w_in=ev_w_in, ev_w_out=ev_w_out, rwkv_mu=rwkv_mu, rwkv_w0=rwkv_w0,
             rwkv_w_up=rwkv_w_up, rwkv_a0=rwkv_a0, rwkv_a_up=rwkv_a_up, rwkv_g_up=rwkv_g_up,
             rwkv_k_k=rwkv_k_k, rwkv_k_a=rwkv_k_a, rwkv_r_k=rwkv_r_k.reshape(rwkv_r_k.shape[0], -1),
             rwkv_ln_w=rwkv_ln_w, rwkv_ln_b=rwkv_ln_b, conv_w=conv_w, od_w_in=od_w_in,
             od_w_out=od_w_out, gla_a_up=gla_a_up, gla_a_b=gla_a_b, gla_norm=gla_norm,
             final_norm=final_norm)
    layers, fnorm = _prepare(w)
    n_even, n_odd = state_rwkv_shift.shape[0], state_gla.shape[0]
    bp = x_prompt.shape[0]
    zeros = lambda *s: jnp.zeros(s, F32)
    y_p, p_shift, p_wkv, p_conv, p_gla = _trunk(
        x_prompt, zeros(n_even, bp, A_COLS), zeros(n_even, bp, A_HEADS, A_HEAD_DIM, A_HEAD_DIM),
        zeros(n_even, bp, CONV_W - 1, D_B), zeros(n_odd, bp, C_HEADS, C_DK, C_DV),
        layers, fnorm, *_tiles(*x_prompt.shape[:2]))
    y_s, s_shift, s_wkv, s_conv, s_gla = _trunk(
        x_sample, state_rwkv_shift, state_rwkv_wkv, state_conv, state_gla,
        layers, fnorm, *_tiles(*x_sample.shape[:2]))
    return (y_p, y_s, p_shift, p_wkv, p_conv, p_gla, s_shift, s_wkv, s_conv, s_gla)
```

```python
import functools

import jax
import jax.numpy as jnp
from jax import lax
from jax.experimental import pallas as pl
from jax.experimental.pallas import tpu as pltpu

F32 = jnp.float32
BF16 = jnp.bfloat16

D_MODEL = 1024
D_FF = 2816
FFN_RES = 0.5
A_HEADS, A_HEAD_DIM = 8, 64
D_A = A_HEADS * A_HEAD_DIM
DECAY_RANK, ICL_RANK, GATE_RANK = 64, 64, 128
LOW_RANK = DECAY_RANK + ICL_RANK + GATE_RANK
A_COLS = 3 * D_A + LOW_RANK
D_B = D_MODEL - D_A
CONV_W = 3
EVEN_IN = A_COLS + 3 * D_B
C_HEADS, C_DK, C_DV = 4, 128, 256
C_KEY, C_VAL = C_HEADS * C_DK, C_HEADS * C_DV
GLA_GATE_RANK = 16
GLA_GATE_NORM = 16.0
ODD_MAIN = 2 * C_KEY + 2 * C_VAL
RMS_EPS = 1e-6
GN_EPS = 64e-5
L2_EPS = 1e-12
DECAY_SCALE = 0.6065306597126334

LANES = 128
MXU_WIDTH = 256
FF_CHUNK = MXU_WIDTH
POST_HALF_MIN = 512
ODD_IN_PAD = ODD_MAIN + LANES
VMEM_LIMIT = 56 * 1024 * 1024


def _dot(a, b):
    return jnp.dot(a, b, preferred_element_type=F32)


def _dot_nt(a, b):
    return lax.dot_general(a, b, (((1,), (1,)), ((), ())), preferred_element_type=F32)


def _dot_tn(a, b):
    return lax.dot_general(a, b, (((0,), (0,)), ((), ())), preferred_element_type=F32)


def _split(x, parts):
    out = []
    for _ in range(parts - 1):
        p = x.astype(BF16)
        out.append(p)
        x = x - p.astype(F32)
    out.append(x.astype(BF16))
    return out


def _cumsum_rows(tri_bf16, x):
    acc = None
    for p in _split(x, 2):
        t = _dot(tri_bf16, p)
        acc = t if acc is None else acc + t
    return acc


def _head_sums(x, bd_bf16, parts):
    w = bd_bf16.shape[0]
    pieces = _split(x, parts)
    tiles = []
    for j in range(x.shape[1] // w):
        acc = None
        for p in pieces:
            t = _dot(p[:, j * w:(j + 1) * w], bd_bf16)
            acc = t if acc is None else acc + t
        tiles.append(acc)
    return jnp.concatenate(tiles, axis=1)


def _rms(x, g):
    return x * lax.rsqrt(jnp.mean(x * x, axis=-1, keepdims=True) + RMS_EPS) * g


def _log_sigmoid(x):
    return jnp.minimum(x, 0.0) - jnp.log(1.0 + jnp.exp(-jnp.abs(x)))


def _tril_masks(c):
    row = lax.broadcasted_iota(jnp.int32, (c, c), 0)
    col = lax.broadcasted_iota(jnp.int32, (c, c), 1)
    return row > col, row >= col


def _ffn_residual(x, g_ref, wg_ref, wu_ref, wd_ref):
    h = _rms(x, g_ref[...]).astype(BF16)
    acts = []
    for start in range(0, D_FF, FF_CHUNK):
        cols = slice(start, min(start + FF_CHUNK, D_FF))
        a = _dot(h, wg_ref[:, cols])
        b = _dot(h, wu_ref[:, cols])
        acts.append((a * jax.nn.sigmoid(a) * b).astype(BF16))
    return x + FFN_RES * _dot(jnp.concatenate(acts, axis=1), wd_ref[...])


def _pre_kernel(x_ref, g_ref, wg_ref, wu_ref, wd_ref, gm_ref, win_ref, x1_ref, proj_ref):
    x1 = _ffn_residual(x_ref[...], g_ref, wg_ref, wu_ref, wd_ref)
    x1_ref[...] = x1
    tm = x1.shape[0]
    g_lo, g_hi = 2 * C_KEY + C_VAL, ODD_MAIN
    for rows in ([slice(0, tm // 2), slice(tm // 2, tm)] if tm % 16 == 0 else [slice(0, tm)]):
        hm = _rms(x1[rows], gm_ref[...]).astype(BF16)
        p = _dot(hm, win_ref[...])
        g = p[:, g_lo:g_hi]
        proj_ref[rows, :g_lo] = p[:, :g_lo]
        proj_ref[rows, g_lo:g_hi] = g * jax.nn.sigmoid(g)
        proj_ref[rows, g_hi:] = p[:, g_hi:]


def _pre_even_kernel(x_ref, g_ref, wg_ref, wu_ref, wd_ref, gm_ref, win_ref,
                     shift0_ref, conv0_ref, mu_ref, cw_ref,
                     x1_ref, xs_ref, yb_ref, shift_ref, conv_ref, sh_s, cv_s, *, tm, t):
    per_tile = tm // t
    if per_tile <= 1:
        @pl.when((pl.program_id(0) * tm) % t == 0)
        def _():
            sh_s[...] = shift0_ref[...]
            cv_s[...] = conv0_ref[...]
        starts = [(0, sh_s[...], cv_s[...])]
    else:
        starts = [(s * t, shift0_ref[s], conv0_ref[s]) for s in range(per_tile)]

    x1 = _ffn_residual(x_ref[...], g_ref, wg_ref, wu_ref, wd_ref)
    x1_ref[...] = x1
    halves = [slice(0, tm // 2), slice(tm // 2, tm)] if tm % 16 == 0 else [slice(0, tm)]
    hm = [_rms(x1[rows], gm_ref[...]).astype(BF16) for rows in halves]
    row = lax.broadcasted_iota(jnp.int32, (tm, 1), 0)

    pa = jnp.concatenate([_dot(h, win_ref[:, :A_COLS]) for h in hm], axis=0)
    prev = pltpu.roll(pa, 1, 0)
    for r0, sh, _ in starts:
        prev = jnp.where(row == r0, sh, prev)
    xs_ref[...] = pa + (prev - pa) * mu_ref[...]

    pb = jnp.concatenate([_dot(h, win_ref[:, A_COLS:]) for h in hm], axis=0)
    u = pb[:, D_B:2 * D_B] * pb[:, 2 * D_B:]
    u1 = pltpu.roll(u, 1, 0)
    u2 = pltpu.roll(u, 2, 0)
    for r0, _, cv in starts:
        u1 = jnp.where(row == r0, cv[1:2, :], u1)
        u2 = jnp.where(row == r0, cv[0:1, :], jnp.where(row == r0 + 1, cv[1:2, :], u2))
    conv = cw_ref[0:1, :] * u2 + cw_ref[1:2, :] * u1 + cw_ref[2:3, :] * u
    yb_ref[...] = (pb[:, :D_B] * conv).astype(BF16)

    if per_tile <= 1:
        sh_s[...] = shift_ref[...] = pa[tm - 1:tm, :]
        cv_s[...] = conv_ref[...] = u[tm - 2:tm, :]
    else:
        for s in range(per_tile):
            shift_ref[s] = pa[(s + 1) * t - 1:(s + 1) * t, :]
            conv_ref[s] = u[(s + 1) * t - 2:(s + 1) * t, :]


def _post_kernel(x_ref, *refs, n_parts, final):
    y_refs = refs[:n_parts]
    wout_ref, g_ref, wg_ref, wu_ref, wd_ref, gf_ref, out_ref = refs[n_parts:]
    tm = x_ref.shape[0]
    halves = [slice(0, tm // 2), slice(tm // 2, tm)] if tm >= 2 * POST_HALF_MIN else [slice(0, tm)]
    x2 = []
    for rows in halves:
        mix, off = None, 0
        for y_ref in y_refs:
            part = _dot(y_ref[rows, :], wout_ref[off:off + y_ref.shape[1], :])
            mix = part if mix is None else mix + part
            off += y_ref.shape[1]
        x2.append(x_ref[rows, :] + mix)
    for rows, v in zip(halves, x2):
        x3 = _ffn_residual(v, g_ref, wg_ref, wu_ref, wd_ref)
        if final:
            x3 = _rms(x3, gf_ref[...])
        out_ref[rows, :] = x3


def _resident(w):
    a, idx = w if isinstance(w, tuple) else (w, ())
    tail = a.shape[len(idx):]
    spec = pl.BlockSpec((None,) * len(idx) + tail, lambda *_: idx + (0,) * len(tail),
                        pipeline_mode=pl.Buffered(1))
    return spec, a


def _row_tile(tm, width):
    return pl.BlockSpec((tm, width), lambda i: (i, 0))


def _pre_call(x, ffn, gm, w_in, tm):
    n = x.shape[0]
    p = w_in[0].shape[-1]
    specs, weights = zip(*[_resident(a) for a in (*ffn, gm, w_in)])
    return pl.pallas_call(
        _pre_kernel,
        grid=(n // tm,),
        in_specs=[_row_tile(tm, D_MODEL), *specs],
        out_specs=[_row_tile(tm, D_MODEL), _row_tile(tm, p)],
        out_shape=[jax.ShapeDtypeStruct((n, D_MODEL), F32), jax.ShapeDtypeStruct((n, p), F32)],
        compiler_params=pltpu.CompilerParams(dimension_semantics=("arbitrary",),
                                             vmem_limit_bytes=VMEM_LIMIT),
        name="pre_ffn_proj",
    )(x, *weights)


def _pre_even_call(x, ffn, gm, w_in, shift0, conv0, mu, conv_w, tm, t):
    n = x.shape[0]
    bsz = n // t
    assert t % tm == 0 or tm % t == 0
    if tm <= t:
        per_seq = lambda *tail: pl.BlockSpec((None,) + tail, lambda i: ((i * tm) // t,) + (0,) * len(tail))
    else:
        per_seq = lambda *tail: pl.BlockSpec((tm // t,) + tail, lambda i: (i,) + (0,) * len(tail))
    state_specs = [per_seq(1, A_COLS), per_seq(CONV_W - 1, D_B)]
    specs, weights = zip(*[_resident(a) for a in (*ffn, gm, w_in)])
    tail_specs, tail_weights = zip(*[_resident(a) for a in (mu, conv_w)])
    return pl.pallas_call(
        functools.partial(_pre_even_kernel, tm=tm, t=t),
        grid=(n // tm,),
        in_specs=[_row_tile(tm, D_MODEL), *specs, *state_specs, *tail_specs],
        out_specs=[_row_tile(tm, D_MODEL), _row_tile(tm, A_COLS), _row_tile(tm, D_B)] + state_specs,
        out_shape=[jax.ShapeDtypeStruct((n, D_MODEL), F32), jax.ShapeDtypeStruct((n, A_COLS), F32),
                   jax.ShapeDtypeStruct((n, D_B), BF16),
                   jax.ShapeDtypeStruct((bsz, 1, A_COLS), F32),
                   jax.ShapeDtypeStruct((bsz, CONV_W - 1, D_B), F32)],
        scratch_shapes=[pltpu.VMEM((1, A_COLS), F32), pltpu.VMEM((CONV_W - 1, D_B), F32)],
        compiler_params=pltpu.CompilerParams(dimension_semantics=("arbitrary",),
                                             vmem_limit_bytes=VMEM_LIMIT),
        name="pre_ffn_proj_shift_conv",
    )(x, *weights, shift0, conv0, *tail_weights)


def _post_call(x, ys, w_out, ffn, gf, tm, final):
    n = x.shape[0]
    specs, weights = zip(*[_resident(a) for a in (w_out, *ffn, gf)])
    return pl.pallas_call(
        functools.partial(_post_kernel, n_parts=len(ys), final=final),
        grid=(n // tm,),
        in_specs=[_row_tile(tm, D_MODEL)] + [_row_tile(tm, y.shape[1]) for y in ys] + list(specs),
        out_specs=_row_tile(tm, D_MODEL),
        out_shape=jax.ShapeDtypeStruct((n, D_MODEL), F32),
        compiler_params=pltpu.CompilerParams(dimension_semantics=("arbitrary",),
                                             vmem_limit_bytes=VMEM_LIMIT),
        name="post_proj_ffn",
    )(x, *ys, *weights)


HEAD_GROUP = MXU_WIDTH // A_HEAD_DIM


def _head_block(h):
    q, j = divmod(h, HEAD_GROUP)
    return q, slice(j * A_HEAD_DIM, (j + 1) * A_HEAD_DIM)


def _blockdiag(x, lane_head):
    zero = jnp.zeros_like(x)
    return jnp.concatenate([jnp.where(lane_head == h, x, zero) for h in range(HEAD_GROUP)], axis=0)


def _feat_head():
    return lax.broadcasted_iota(jnp.int32, (1, MXU_WIDTH), 1) // A_HEAD_DIM


def _rwkv_prepare(chunks, c):
    _, incl_cc = _tril_masks(c)
    tri = jnp.where(incl_cc, 1.0, 0.0).astype(BF16)
    hw = HEAD_GROUP * c
    row = lax.broadcasted_iota(jnp.int32, (c, hw), 0)
    col = lax.broadcasted_iota(jnp.int32, (c, hw), 1) % c
    strict, incl = row > col, row >= col
    feat_head = _feat_head()
    frame_head = lax.broadcasted_iota(jnp.int32, (1, hw), 1) // c
    units = []
    for ci, ins in enumerate(chunks):
        for i, (r, l, k, v, kk, b) in enumerate(ins):
            cum = _cumsum_rows(tri, l)
            ctot = cum[c - 1:c, :]
            e_neg = jnp.exp(-cum)
            e_end = jnp.exp(ctot - cum)
            kap = kk * jnp.exp(cum - l)
            rt = r * jnp.exp(cum)
            bt = (b * e_neg).astype(BF16)
            kt = (k * e_neg).astype(BF16)
            b2e = b * e_end
            k2e = k * e_end
            gam = jnp.exp(ctot)
            for q in range(D_A // MXU_WIDTH):
                sl = slice(q * MXU_WIDTH, (q + 1) * MXU_WIDTH)
                units.append(dict(
                    chunk=ci, seq=i, group=q, gam=gam[:, sl], v=v[:, sl],
                    lhs=jnp.concatenate([kap[:, sl], rt[:, sl]], axis=0).astype(BF16),
                    rhs=jnp.concatenate([_blockdiag(bt[:, sl], feat_head),
                                         _blockdiag(kt[:, sl], feat_head)], axis=0),
                    kall=jnp.concatenate([k2e[:, sl], b2e[:, sl]], axis=0).astype(BF16)))
    g = [_dot_nt(u["lhs"], u["rhs"]) for u in units]
    for u, gj in zip(units, g):
        a_kq = jnp.concatenate([jnp.where(strict, gj[:c, hw:], 0.0), jnp.where(incl, gj[c:, hw:], 0.0)],
                               axis=0).astype(BF16)
        u["kv"] = _dot(a_kq, _blockdiag(u["v"].astype(BF16), feat_head))
        u["q_b"] = jnp.where(incl, gj[c:, :hw], 0.0).astype(BF16)
    n = range(len(units))
    t = [jnp.where(strict, -gj[:c, :hw], 0.0) for gj in g]
    m = [_dot(x.astype(BF16), _blockdiag(x.astype(BF16), frame_head)) for x in t]
    for _ in range(c.bit_length() - 3):
        both = [_dot(jnp.concatenate([t[j], m[j]], axis=0).astype(BF16),
                     _blockdiag(m[j].astype(BF16), frame_head)) for j in n]
        t = [t[j] + m[j] + both[j][:c] for j in n]
        m = [both[j][c:] for j in n]
    t = [(t[j] + m[j] + _dot(t[j].astype(BF16), _blockdiag(m[j].astype(BF16), frame_head))).astype(BF16)
         for j in n]
    for u, tj in zip(units, t):
        kap_b, kv_top = u["lhs"][:c], u["kv"][:c]
        khat = kap_b.astype(F32) + _dot(tj, _blockdiag(kap_b, feat_head))
        u["lhs"] = jnp.concatenate([khat.astype(BF16), u["lhs"][c:]], axis=0)
        u["tkv"] = kv_top + _dot(tj, _blockdiag(kv_top.astype(BF16), feat_head))
    return [[u for u in units if u["chunk"] == ci] for ci in range(len(chunks))]


def _rwkv_advance(s_ref, units, c, n_seq):
    feat_head = _feat_head()
    state_mask = (lax.broadcasted_iota(jnp.int32, (MXU_WIDTH, MXU_WIDTH), 0) // A_HEAD_DIM
                  == lax.broadcasted_iota(jnp.int32, (MXU_WIDTH, MXU_WIDTH), 1) // A_HEAD_DIM)
    s0 = [s_ref[u["seq"], u["group"]] for u in units]
    p = [_dot_nt(u["lhs"], s.astype(BF16)) for u, s in zip(units, s0)]
    uu = [pj[:c] + u["tkv"] for u, pj in zip(units, p)]
    ys = [pj[c:] + u["kv"][c:] - _dot(u["q_b"], _blockdiag(uj.astype(BF16), feat_head))
          for u, pj, uj in zip(units, p, uu)]
    for u, s, uj in zip(units, s0, uu):
        z = jnp.concatenate([u["v"], -uj], axis=0).astype(BF16)
        s_ref[u["seq"], u["group"]] = s * u["gam"] + jnp.where(state_mask, _dot_tn(z, u["kall"]), 0.0)
    return [jnp.concatenate([y for u, y in zip(units, ys) if u["seq"] == i], axis=1) for i in range(n_seq)]


def _even_kernel(xs_ref, wkv0_ref, w0_ref, a0_ref, wlr_ref, kk_ref, ka_ref, rk_ref, lnw_ref, lnb_ref,
                 bd_ref, y_ref, wkv_ref, s_ref, *, bg, tb, c):
    t_idx = pl.program_id(1)

    @pl.when(t_idx == 0)
    def _():
        s_ref[...] = jnp.zeros_like(s_ref)
        for i in range(bg):
            for h in range(A_HEADS):
                q, d = _head_block(h)
                s_ref[i, q, d, d] = wkv0_ref[i, h]

    lane = lax.broadcasted_iota(jnp.int32, (1, LOW_RANK), 1)
    bd = bd_ref[...]
    inv_n = 1.0 / A_HEAD_DIM
    seqs = range(bg)
    stack = lambda xs: jnp.concatenate(xs, axis=0)
    chunks, keep = [], []
    for ci in range(tb // c):
        rows = slice(ci * c, (ci + 1) * c)
        xs = [xs_ref[i, rows, :] for i in seqs]
        lo = stack([x[:, 3 * D_A:] for x in xs])
        lo_act = jnp.where(lane < DECAY_RANK, jnp.tanh(lo),
                           jnp.where(lane < DECAY_RANK + ICL_RANK, lo, jax.nn.sigmoid(lo)))
        z = _dot(lo_act.astype(BF16), wlr_ref[...])
        r = stack([x[:, :D_A] for x in xs])
        k = stack([x[:, D_A:2 * D_A] for x in xs])
        v = stack([x[:, 2 * D_A:3 * D_A] for x in xs])
        log_decay = jax.nn.sigmoid(w0_ref[...] + z[:, :D_A]) * (-DECAY_SCALE)
        a = jax.nn.sigmoid(a0_ref[...] + z[:, D_A:2 * D_A])
        kk = k * kk_ref[...]
        kk = kk * jnp.minimum(lax.rsqrt(_head_sums(kk * kk, bd, 1)), 1.0 / L2_EPS)
        k_eff = k * (1.0 + (a - 1.0) * ka_ref[...])
        b = kk * a
        chunks.append([tuple(x[i * c:(i + 1) * c] for x in (r, log_decay, k_eff, v, kk, b)) for i in seqs])
        keep.append((r * k_eff * rk_ref[...], v, z[:, 2 * D_A:]))
    prepared = _rwkv_prepare(chunks, c)
    for ci, (rk, v, gate) in enumerate(keep):
        rows = slice(ci * c, (ci + 1) * c)
        y = stack(_rwkv_advance(s_ref, prepared[ci], c, bg))
        d = y - _head_sums(y, bd, 1) * inv_n
        var = _head_sums(d * d, bd, 1) * inv_n
        yn = d * lax.rsqrt(var + GN_EPS) * lnw_ref[...] + lnb_ref[...]
        bonus = _head_sums(rk, bd, 1) * v
        y_a = ((yn + bonus) * gate).astype(BF16)
        for i in seqs:
            y_ref[i, rows, :] = y_a[i * c:(i + 1) * c]

    @pl.when(t_idx == pl.num_programs(1) - 1)
    def _():
        for i in range(bg):
            for h in range(A_HEADS):
                q, d = _head_block(h)
                wkv_ref[i, h] = s_ref[i, q, d, d]


def _even_call(xs, wkv0, prm, bg, tb, c):
    bsz, t, _ = xs.shape
    params = (prm["w0"], prm["a0"], prm["w_lr"], prm["k_k"], prm["k_a"], prm["r_k"],
              prm["ln_w"], prm["ln_b"], prm["bd"])
    state_spec = pl.BlockSpec((bg, A_HEADS, A_HEAD_DIM, A_HEAD_DIM), lambda b, i: (b, 0, 0, 0))
    return pl.pallas_call(
        functools.partial(_even_kernel, bg=bg, tb=tb, c=c),
        grid=(bsz // bg, t // tb),
        in_specs=[pl.BlockSpec((bg, tb, A_COLS), lambda b, i: (b, i, 0)), state_spec]
        + [pl.BlockSpec(a.shape, lambda b, i, nd=a.ndim: (0,) * nd) for a in params],
        out_specs=[pl.BlockSpec((bg, tb, D_A), lambda b, i: (b, i, 0)), state_spec],
        out_shape=[jax.ShapeDtypeStruct((bsz, t, D_A), BF16),
                   jax.ShapeDtypeStruct((bsz, A_HEADS, A_HEAD_DIM, A_HEAD_DIM), F32)],
        scratch_shapes=[pltpu.VMEM((bg, D_A // MXU_WIDTH, MXU_WIDTH, MXU_WIDTH), F32)],
        compiler_params=pltpu.CompilerParams(dimension_semantics=("arbitrary", "arbitrary"),
                                             vmem_limit_bytes=VMEM_LIMIT),
        name="even_mixer",
    )(xs, wkv0, *params)


def _odd_kernel(proj_ref, gla0_ref, aup_ref, ab_ref, gn_ref, y_ref, gla_ref, st_ref, *, bg, tb, c):
    t_idx = pl.program_id(1)

    @pl.when(t_idx == 0)
    def _():
        st_ref[...] = gla0_ref[...]

    _, incl = _tril_masks(c)
    tri = jnp.where(incl, 1.0, 0.0).astype(BF16)
    gn = gn_ref[...]
    scale = C_DK ** -0.5
    units = []
    for i in range(bg):
        a_lo = proj_ref[i, :, ODD_MAIN:]
        la = _log_sigmoid(_dot(a_lo.astype(BF16), aup_ref[...]) + ab_ref[...]) * (1.0 / GLA_GATE_NORM)
        for ci in range(tb // c):
            rows = slice(ci * c, (ci + 1) * c)
            cum = _cumsum_rows(tri, la[rows])
            ctot = cum[c - 1:c, :]
            k = proj_ref[i, rows, C_KEY:2 * C_KEY]
            qd = (proj_ref[i, rows, :C_KEY] * scale * jnp.exp(cum)).astype(BF16)
            kd = (k * jnp.exp(-cum)).astype(BF16)
            k2e = (k * jnp.exp(ctot - cum)).astype(BF16)
            gam = jnp.exp(ctot)
            for h in range(C_HEADS):
                ks = slice(h * C_DK, (h + 1) * C_DK)
                cols = slice(2 * C_KEY + h * C_DV, 2 * C_KEY + (h + 1) * C_DV)
                units.append(dict(seq=i, chunk=ci, head=h, rows=rows, qd=qd[:, ks], kd=kd[:, ks],
                                  k2e=k2e[:, ks], gam=gam[:, ks],
                                  v=proj_ref[i, rows, cols].astype(BF16)))
    scores = [jnp.where(incl, _dot_nt(u["qd"], u["kd"]), 0.0).astype(BF16) for u in units]
    intra = [_dot(s, u["v"]) for s, u in zip(scores, units)]
    upd = [_dot_tn(u["v"], u["k2e"]) for u in units]
    state = {(i, h): st_ref[i, h] for i in range(bg) for h in range(C_HEADS)}
    inter = []
    for ci in range(tb // c):
        for u, up in zip(units, upd):
            if u["chunk"] == ci:
                key = (u["seq"], u["head"])
                inter.append((u, _dot_nt(u["qd"], state[key].astype(BF16))))
                state[key] = state[key] * u["gam"] + up
    for key, s in state.items():
        st_ref[key] = s
    inter = {(u["seq"], u["chunk"], u["head"]): x for u, x in inter}
    for u, a in zip(units, intra):
        o = a + inter[(u["seq"], u["chunk"], u["head"])]
        o = o * lax.rsqrt(jnp.mean(o * o, axis=-1, keepdims=True) + RMS_EPS) * gn
        gcols = slice(2 * C_KEY + C_VAL + u["head"] * C_DV, 2 * C_KEY + C_VAL + (u["head"] + 1) * C_DV)
        gate = proj_ref[u["seq"], u["rows"], gcols]
        y_ref[u["seq"], u["rows"], u["head"] * C_DV:(u["head"] + 1) * C_DV] = (o * gate).astype(BF16)

    @pl.when(t_idx == pl.num_programs(1) - 1)
    def _():
        gla_ref[...] = st_ref[...]


def _odd_call(proj, gla0_t, prm, bg, tb, c):
    bsz, t, _ = proj.shape
    params = (prm["a_up"], prm["a_b"], prm["gn"])
    state_spec = pl.BlockSpec((bg, C_HEADS, C_DV, C_DK), lambda b, i: (b, 0, 0, 0))
    return pl.pallas_call(
        functools.partial(_odd_kernel, bg=bg, tb=tb, c=c),
        grid=(bsz // bg, t // tb),
        in_specs=[pl.BlockSpec((bg, tb, ODD_IN_PAD), lambda b, i: (b, i, 0)), state_spec]
        + [pl.BlockSpec(a.shape, lambda b, i, nd=a.ndim: (0,) * nd) for a in params],
        out_specs=[pl.BlockSpec((bg, tb, D_MODEL), lambda b, i: (b, i, 0)), state_spec],
        out_shape=[jax.ShapeDtypeStruct((bsz, t, D_MODEL), BF16),
                   jax.ShapeDtypeStruct((bsz, C_HEADS, C_DV, C_DK), F32)],
        scratch_shapes=[pltpu.VMEM((bg, C_HEADS, C_DV, C_DK), F32)],
        compiler_params=pltpu.CompilerParams(dimension_semantics=("arbitrary", "arbitrary"),
                                             vmem_limit_bytes=VMEM_LIMIT),
        name="odd_mixer",
    )(proj, gla0_t, *params)


def _prepare(w):
    depth = w["ffn_norm"].shape[0]
    row = lambda a: a.reshape(1, -1).astype(F32)

    wg, wu, wd = (w[k].astype(BF16) for k in ("ffn_w_gate", "ffn_w_up", "ffn_w_down"))
    ev_in, ev_out, od_out = (w[k].astype(BF16) for k in ("ev_w_in", "ev_w_out", "od_w_out"))
    od_in = jnp.pad(w["od_w_in"], ((0, 0), (0, 0), (0, ODD_IN_PAD - w["od_w_in"].shape[-1]))).astype(BF16)

    def ffn(layer, j):
        return (row(w["ffn_norm"][layer, j]), (wg, (layer, j)), (wu, (layer, j)), (wd, (layer, j)))

    head = jnp.arange(MXU_WIDTH) // A_HEAD_DIM
    bd = (head[:, None] == head[None, :]).astype(BF16)
    layers = []
    for layer in range(depth):
        i = layer // 2
        if layer % 2 == 0:
            w_lr = jnp.zeros((LOW_RANK, 3 * D_A), F32)
            w_lr = w_lr.at[:DECAY_RANK, :D_A].set(w["rwkv_w_up"][i])
            w_lr = w_lr.at[DECAY_RANK:DECAY_RANK + ICL_RANK, D_A:2 * D_A].set(w["rwkv_a_up"][i])
            w_lr = w_lr.at[DECAY_RANK + ICL_RANK:, 2 * D_A:].set(w["rwkv_g_up"][i])
            mix = dict(mu=row(w["rwkv_mu"][i]), w0=row(w["rwkv_w0"][i]), a0=row(w["rwkv_a0"][i]),
                       w_lr=w_lr.astype(BF16), k_k=row(w["rwkv_k_k"][i]), k_a=row(w["rwkv_k_a"][i]),
                       r_k=row(w["rwkv_r_k"][i]), ln_w=row(w["rwkv_ln_w"][i]), ln_b=row(w["rwkv_ln_b"][i]),
                       conv_w=w["conv_w"][i].astype(F32), bd=bd)
            w_in, w_out = (ev_in, (i,)), (ev_out, (i,))
        else:
            a_up = jnp.zeros((LANES, C_KEY), F32).at[:GLA_GATE_RANK].set(w["gla_a_up"][i])
            mix = dict(a_up=a_up.astype(BF16), a_b=row(w["gla_a_b"][i]), gn=row(w["gla_norm"][i]))
            w_in, w_out = (od_in, (i,)), (od_out, (i,))
        layers.append(dict(ffn0=ffn(layer, 0), ffn1=ffn(layer, 1), gm=row(w["mix_norm"][layer]),
                           w_in=w_in, w_out=w_out, mix=mix))
    return layers, row(w["final_norm"])


def _trunk(x, shift0, wkv0, conv0, gla0, layers, final_norm, tm, bg, tb, c):
    bsz, t, _ = x.shape
    n = bsz * t
    x = x.reshape(n, D_MODEL)
    new_shift, new_wkv, new_conv, new_gla = [], [], [], []
    for layer, lw in enumerate(layers):
        i = layer // 2
        if layer % 2 == 0:
            x, xs, y_b, sh, cv = _pre_even_call(x, lw["ffn0"], lw["gm"], lw["w_in"], shift0[i][:, None, :],
                                                conv0[i], lw["mix"]["mu"], lw["mix"]["conv_w"], tm, t)
            y_a, wkv = _even_call(xs.reshape(bsz, t, A_COLS), wkv0[i], lw["mix"], bg, tb, c)
            ys = [y_a.reshape(n, D_A), y_b]
            new_shift.append(sh[:, 0, :])
            new_wkv.append(wkv)
            new_conv.append(cv)
        else:
            x, proj = _pre_call(x, lw["ffn0"], lw["gm"], lw["w_in"], tm)
            tb_odd = 2 * tb if t % (2 * tb) == 0 else tb
            mix, st = _odd_call(proj.reshape(bsz, t, -1), jnp.swapaxes(gla0[i], -1, -2), lw["mix"],
                                bg, tb_odd, c)
            ys = [mix.reshape(n, D_MODEL)]
            new_gla.append(jnp.swapaxes(st, -1, -2))
        tm_post = 2 * tm if tm >= 512 and n % (2 * tm) == 0 else tm
        x = _post_call(x, ys, lw["w_out"], lw["ffn1"], final_norm, tm_post,
                       final=layer == len(layers) - 1)
    return (x.reshape(bsz, t, D_MODEL), jnp.stack(new_shift), jnp.stack(new_wkv),
            jnp.stack(new_conv), jnp.stack(new_gla))


def _tiles(bsz, t):
    n = bsz * t
    tm = 512 if t % 512 == 0 else n
    bg = 2 if bsz % 2 == 0 else 1
    tb = 256 if t % 256 == 0 else t
    c = 64 if tb % 64 == 0 else tb
    return tm, bg, tb, c


def kernel(x_prompt, x_sample, state_rwkv_shift, state_rwkv_wkv, state_conv, state_gla, ffn_norm, ffn_w_gate, ffn_w_up, ffn_w_down, mix_norm, ev_w_in, ev_w_out, rwkv_mu, rwkv_w0, rwkv_w_up, rwkv_a0, rwkv_a_up, rwkv_g_up, rwkv_k_k, rwkv_k_a, rwkv_r_k, rwkv_ln_w, rwkv_ln_b, conv_w, od_w_in, od_w_out, gla_a_up, gla_a_b, gla_norm, final_norm):
    w = dict(ffn_norm=ffn_norm, ffn_w_gate=ffn_w_gate, ffn_w_up=ffn_w_up, ffn_w_down=ffn_w_down,
             mix_norm=mix_norm, ev_w_in=ev_w_in, ev_w_out=ev_w_out, rwkv_mu=rwkv_mu, rwkv_w0=rwkv_w0,
             rwkv_w_up=rwkv_w_up, rwkv_a0=rwkv_a0, rwkv_a_up=rwkv_a_up, rwkv_g_up=rwkv_g_up,
             rwkv_k_k=rwkv_k_k, rwkv_k_a=rwkv_k_a, rwkv_r_k=rwkv_r_k.reshape(rwkv_r_k.shape[0], -1),
             rwkv_ln_w=rwkv_ln_w, rwkv_ln_b=rwkv_ln_b, conv_w=conv_w, od_w_in=od_w_in,
             od_w_out=od_w_out, gla_a_up=gla_a_up, gla_a_b=gla_a_b, gla_norm=gla_norm,
             final_norm=final_norm)
    layers, fnorm = _prepare(w)
    n_even, n_odd = state_rwkv_shift.shape[0], state_gla.shape[0]
    bp = x_prompt.shape[0]
    zeros = lambda *s: jnp.zeros(s, F32)
    y_p, p_shift, p_wkv, p_conv, p_gla = _trunk(
        x_prompt, zeros(n_even, bp, A_COLS), zeros(n_even, bp, A_HEADS, A_HEAD_DIM, A_HEAD_DIM),
        zeros(n_even, bp, CONV_W - 1, D_B), zeros(n_odd, bp, C_HEADS, C_DK, C_DV),
        layers, fnorm, *_tiles(*x_prompt.shape[:2]))
    y_s, s_shift, s_wkv, s_conv, s_gla = _trunk(
        x_sample, state_rwkv_shift, state_rwkv_wkv, state_conv, state_gla,
        layers, fnorm, *_tiles(*x_sample.shape[:2]))
    return (y_p, y_s, p_shift, p_wkv, p_conv, p_gla, s_shift, s_wkv, s_conv, s_gla)
```

```python
import functools

import jax
import jax.numpy as jnp
from jax import lax
from jax.experimental import pallas as pl
from jax.experimental.pallas import tpu as pltpu

F32 = jnp.float32
BF16 = jnp.bfloat16

D_MODEL = 1024
D_FF = 2816
FFN_RES = 0.5
A_HEADS, A_HEAD_DIM = 8, 64
D_A = A_HEADS * A_HEAD_DIM
DECAY_RANK, ICL_RANK, GATE_RANK = 64, 64, 128
LOW_RANK = DECAY_RANK + ICL_RANK + GATE_RANK
A_COLS = 3 * D_A + LOW_RANK
D_B = D_MODEL - D_A
CONV_W = 3
EVEN_IN = A_COLS + 3 * D_B
C_HEADS, C_DK, C_DV = 4, 128, 256
C_KEY, C_VAL = C_HEADS * C_DK, C_HEADS * C_DV
GLA_GATE_RANK = 16
GLA_GATE_NORM = 16.0
ODD_MAIN = 2 * C_KEY + 2 * C_VAL
RMS_EPS = 1e-6
GN_EPS = 64e-5
L2_EPS = 1e-12
DECAY_SCALE = 0.6065306597126334

LANES = 128
MXU_WIDTH = 256
FF_CHUNK = MXU_WIDTH
POST_HALF_MIN = 512
ODD_IN_PAD = ODD_MAIN + LANES
VMEM_LIMIT = 56 * 1024 * 1024


def _dot(a, b):
    return jnp.dot(a, b, preferred_element_type=F32)


def _dot_nt(a, b):
    return lax.dot_general(a, b, (((1,), (1,)), ((), ())), preferred_element_type=F32)


def _dot_tn(a, b):
    return lax.dot_general(a, b, (((0,), (0,)), ((), ())), preferred_element_type=F32)


def _split(x, parts):
    out = []
    for _ in range(parts - 1):
        p = x.astype(BF16)
        out.append(p)
        x = x - p.astype(F32)
    out.append(x.astype(BF16))
    return out


def _cumsum_rows(tri_bf16, x):
    acc = None
    for p in _split(x, 2):
        t = _dot(tri_bf16, p)
        acc = t if acc is None else acc + t
    return acc


def _head_sums(x, bd_bf16, parts):
    w = bd_bf16.shape[0]
    pieces = _split(x, parts)
    tiles = []
    for j in range(x.shape[1] // w):
        acc = None
        for p in pieces:
            t = _dot(p[:, j * w:(j + 1) * w], bd_bf16)
            acc = t if acc is None else acc + t
        tiles.append(acc)
    return jnp.concatenate(tiles, axis=1)


def _rms(x, g):
    return x * lax.rsqrt(jnp.mean(x * x, axis=-1, keepdims=True) + RMS_EPS) * g


def _log_sigmoid(x):
    return jnp.minimum(x, 0.0) - jnp.log(1.0 + jnp.exp(-jnp.abs(x)))


def _tril_masks(c):
    row = lax.broadcasted_iota(jnp.int32, (c, c), 0)
    col = lax.broadcasted_iota(jnp.int32, (c, c), 1)
    return row > col, row >= col


def _ffn_residual(x, g_ref, wg_ref, wu_ref, wd_ref):
    h = _rms(x, g_ref[...]).astype(BF16)
    acts = []
    for start in range(0, D_FF, FF_CHUNK):
        cols = slice(start, min(start + FF_CHUNK, D_FF))
        a = _dot(h, wg_ref[:, cols])
        b = _dot(h, wu_ref[:, cols])
        acts.append((a * jax.nn.sigmoid(a) * b).astype(BF16))
    return x + FFN_RES * _dot(jnp.concatenate(acts, axis=1), wd_ref[...])


def _pre_kernel(x_ref, g_ref, wg_ref, wu_ref, wd_ref, gm_ref, win_ref, x1_ref, proj_ref):
    x1 = _ffn_residual(x_ref[...], g_ref, wg_ref, wu_ref, wd_ref)
    x1_ref[...] = x1
    tm = x1.shape[0]
    g_lo, g_hi = 2 * C_KEY + C_VAL, ODD_MAIN
    for rows in ([slice(0, tm // 2), slice(tm // 2, tm)] if tm % 16 == 0 else [slice(0, tm)]):
        hm = _rms(x1[rows], gm_ref[...]).astype(BF16)
        p = _dot(hm, win_ref[...])
        g = p[:, g_lo:g_hi]
        proj_ref[rows, :g_lo] = p[:, :g_lo]
        proj_ref[rows, g_lo:g_hi] = g * jax.nn.sigmoid(g)
        proj_ref[rows, g_hi:] = p[:, g_hi:]


def _pre_even_kernel(x_ref, g_ref, wg_ref, wu_ref, wd_ref, gm_ref, win_ref,
                     shift0_ref, conv0_ref, mu_ref, cw_ref,
                     x1_ref, xs_ref, yb_ref, shift_ref, conv_ref, sh_s, cv_s, *, tm, t):
    per_tile = tm // t
    if per_tile <= 1:
        @pl.when((pl.program_id(0) * tm) % t == 0)
        def _():
            sh_s[...] = shift0_ref[...]
            cv_s[...] = conv0_ref[...]
        starts = [(0, sh_s[...], cv_s[...])]
    else:
        starts = [(s * t, shift0_ref[s], conv0_ref[s]) for s in range(per_tile)]

    x1 = _ffn_residual(x_ref[...], g_ref, wg_ref, wu_ref, wd_ref)
    x1_ref[...] = x1
    halves = [slice(0, tm // 2), slice(tm // 2, tm)] if tm % 16 == 0 else [slice(0, tm)]
    hm = [_rms(x1[rows], gm_ref[...]).astype(BF16) for rows in halves]
    row = lax.broadcasted_iota(jnp.int32, (tm, 1), 0)

    pa = jnp.concatenate([_dot(h, win_ref[:, :A_COLS]) for h in hm], axis=0)
    prev = pltpu.roll(pa, 1, 0)
    for r0, sh, _ in starts:
        prev = jnp.where(row == r0, sh, prev)
    xs = pa + (prev - pa) * mu_ref[...]
    lo = xs[:, 3 * D_A:]
    lane = lax.broadcasted_iota(jnp.int32, (1, LOW_RANK), 1)
    xs_ref[:, :3 * D_A] = xs[:, :3 * D_A]
    xs_ref[:, 3 * D_A:] = jnp.where(lane < DECAY_RANK, jnp.tanh(lo),
                                    jnp.where(lane < DECAY_RANK + ICL_RANK, lo, jax.nn.sigmoid(lo)))

    pb = jnp.concatenate([_dot(h, win_ref[:, A_COLS:]) for h in hm], axis=0)
    u = pb[:, D_B:2 * D_B] * pb[:, 2 * D_B:]
    u1 = pltpu.roll(u, 1, 0)
    u2 = pltpu.roll(u, 2, 0)
    for r0, _, cv in starts:
        u1 = jnp.where(row == r0, cv[1:2, :], u1)
        u2 = jnp.where(row == r0, cv[0:1, :], jnp.where(row == r0 + 1, cv[1:2, :], u2))
    conv = cw_ref[0:1, :] * u2 + cw_ref[1:2, :] * u1 + cw_ref[2:3, :] * u
    yb_ref[...] = (pb[:, :D_B] * conv).astype(BF16)

    if per_tile <= 1:
        sh_s[...] = shift_ref[...] = pa[tm - 1:tm, :]
        cv_s[...] = conv_ref[...] = u[tm - 2:tm, :]
    else:
        for s in range(per_tile):
            shift_ref[s] = pa[(s + 1) * t - 1:(s + 1) * t, :]
            conv_ref[s] = u[(s + 1) * t - 2:(s + 1) * t, :]


def _post_kernel(x_ref, *refs, n_parts, final):
    y_refs = refs[:n_parts]
    wout_ref, g_ref, wg_ref, wu_ref, wd_ref, gf_ref, out_ref = refs[n_parts:]
    tm = x_ref.shape[0]
    halves = [slice(0, tm // 2), slice(tm // 2, tm)] if tm >= 2 * POST_HALF_MIN else [slice(0, tm)]
    x2 = []
    for rows in halves:
        mix, off = None, 0
        for y_ref in y_refs:
            part = _dot(y_ref[rows, :], wout_ref[off:off + y_ref.shape[1], :])
            mix = part if mix is None else mix + part
            off += y_ref.shape[1]
        x2.append(x_ref[rows, :] + mix)
    for rows, v in zip(halves, x2):
        x3 = _ffn_residual(v, g_ref, wg_ref, wu_ref, wd_ref)
        if final:
            x3 = _rms(x3, gf_ref[...])
        out_ref[rows, :] = x3


def _resident(w):
    a, idx = w if isinstance(w, tuple) else (w, ())
    tail = a.shape[len(idx):]
    spec = pl.BlockSpec((None,) * len(idx) + tail, lambda *_: idx + (0,) * len(tail),
                        pipeline_mode=pl.Buffered(1))
    return spec, a


def _row_tile(tm, width):
    return pl.BlockSpec((tm, width), lambda i: (i, 0))


def _pre_call(x, ffn, gm, w_in, tm):
    n = x.shape[0]
    p = w_in[0].shape[-1]
    specs, weights = zip(*[_resident(a) for a in (*ffn, gm, w_in)])
    return pl.pallas_call(
        _pre_kernel,
        grid=(n // tm,),
        in_specs=[_row_tile(tm, D_MODEL), *specs],
        out_specs=[_row_tile(tm, D_MODEL), _row_tile(tm, p)],
        out_shape=[jax.ShapeDtypeStruct((n, D_MODEL), F32), jax.ShapeDtypeStruct((n, p), F32)],
        compiler_params=pltpu.CompilerParams(dimension_semantics=("arbitrary",),
                                             vmem_limit_bytes=VMEM_LIMIT),
        name="pre_ffn_proj",
    )(x, *weights)


def _pre_even_call(x, ffn, gm, w_in, shift0, conv0, mu, conv_w, tm, t):
    n = x.shape[0]
    bsz = n // t
    assert t % tm == 0 or tm % t == 0
    if tm <= t:
        per_seq = lambda *tail: pl.BlockSpec((None,) + tail, lambda i: ((i * tm) // t,) + (0,) * len(tail))
    else:
        per_seq = lambda *tail: pl.BlockSpec((tm // t,) + tail, lambda i: (i,) + (0,) * len(tail))
    state_specs = [per_seq(1, A_COLS), per_seq(CONV_W - 1, D_B)]
    specs, weights = zip(*[_resident(a) for a in (*ffn, gm, w_in)])
    tail_specs, tail_weights = zip(*[_resident(a) for a in (mu, conv_w)])
    return pl.pallas_call(
        functools.partial(_pre_even_kernel, tm=tm, t=t),
        grid=(n // tm,),
        in_specs=[_row_tile(tm, D_MODEL), *specs, *state_specs, *tail_specs],
        out_specs=[_row_tile(tm, D_MODEL), _row_tile(tm, A_COLS), _row_tile(tm, D_B)] + state_specs,
        out_shape=[jax.ShapeDtypeStruct((n, D_MODEL), F32), jax.ShapeDtypeStruct((n, A_COLS), F32),
                   jax.ShapeDtypeStruct((n, D_B), BF16),
                   jax.ShapeDtypeStruct((bsz, 1, A_COLS), F32),
                   jax.ShapeDtypeStruct((bsz, CONV_W - 1, D_B), F32)],
        scratch_shapes=[pltpu.VMEM((1, A_COLS), F32), pltpu.VMEM((CONV_W - 1, D_B), F32)],
        compiler_params=pltpu.CompilerParams(dimension_semantics=("arbitrary",),
                                             vmem_limit_bytes=VMEM_LIMIT),
        name="pre_ffn_proj_shift_conv",
    )(x, *weights, shift0, conv0, *tail_weights)


def _post_call(x, ys, w_out, ffn, gf, tm, final):
    n = x.shape[0]
    specs, weights = zip(*[_resident(a) for a in (w_out, *ffn, gf)])
    return pl.pallas_call(
        functools.partial(_post_kernel, n_parts=len(ys), final=final),
        grid=(n // tm,),
        in_specs=[_row_tile(tm, D_MODEL)] + [_row_tile(tm, y.shape[1]) for y in ys] + list(specs),
        out_specs=_row_tile(tm, D_MODEL),
        out_shape=jax.ShapeDtypeStruct((n, D_MODEL), F32),
        compiler_params=pltpu.CompilerParams(dimension_semantics=("arbitrary",),
                                             vmem_limit_bytes=VMEM_LIMIT),
        name="post_proj_ffn",
    )(x, *ys, *weights)


HEAD_GROUP = MXU_WIDTH // A_HEAD_DIM


def _head_block(h):
    q, j = divmod(h, HEAD_GROUP)
    return q, slice(j * A_HEAD_DIM, (j + 1) * A_HEAD_DIM)


def _blockdiag(x, lane_head):
    zero = jnp.zeros_like(x)
    return jnp.concatenate([jnp.where(lane_head == h, x, zero) for h in range(HEAD_GROUP)], axis=0)


def _feat_head():
    return lax.broadcasted_iota(jnp.int32, (1, MXU_WIDTH), 1) // A_HEAD_DIM


def _rwkv_prepare(chunks, c):
    _, incl_cc = _tril_masks(c)
    tri = jnp.where(incl_cc, 1.0, 0.0).astype(BF16)
    hw = HEAD_GROUP * c
    row = lax.broadcasted_iota(jnp.int32, (c, hw), 0)
    col = lax.broadcasted_iota(jnp.int32, (c, hw), 1) % c
    strict, incl = row > col, row >= col
    feat_head = _feat_head()
    frame_head = lax.broadcasted_iota(jnp.int32, (1, hw), 1) // c
    units = []
    for ci, ins in enumerate(chunks):
        for i, (r, l, k, v, kk, b) in enumerate(ins):
            cum = _cumsum_rows(tri, l)
            ctot = cum[c - 1:c, :]
            e_neg = jnp.exp(-cum)
            e_end = jnp.exp(ctot - cum)
            kap = kk * jnp.exp(cum - l)
            rt = r * jnp.exp(cum)
            bt = (b * e_neg).astype(BF16)
            kt = (k * e_neg).astype(BF16)
            b2e = b * e_end
            k2e = k * e_end
            gam = jnp.exp(ctot)
            for q in range(D_A // MXU_WIDTH):
                sl = slice(q * MXU_WIDTH, (q + 1) * MXU_WIDTH)
                units.append(dict(
                    chunk=ci, seq=i, group=q, gam=gam[:, sl], v=v[:, sl],
                    lhs=jnp.concatenate([kap[:, sl], rt[:, sl]], axis=0).astype(BF16),
                    rhs=jnp.concatenate([_blockdiag(bt[:, sl], feat_head),
                                         _blockdiag(kt[:, sl], feat_head)], axis=0),
                    kall=jnp.concatenate([k2e[:, sl], b2e[:, sl]], axis=0).astype(BF16)))
    g = [_dot_nt(u["lhs"], u["rhs"]) for u in units]
    for u, gj in zip(units, g):
        a_kq = jnp.concatenate([jnp.where(strict, gj[:c, hw:], 0.0), jnp.where(incl, gj[c:, hw:], 0.0)],
                               axis=0).astype(BF16)
        u["kv"] = _dot(a_kq, _blockdiag(u["v"].astype(BF16), feat_head))
        u["q_b"] = jnp.where(incl, gj[c:, :hw], 0.0).astype(BF16)
    n = range(len(units))
    t = [jnp.where(strict, -gj[:c, :hw], 0.0) for gj in g]
    m = [_dot(x.astype(BF16), _blockdiag(x.astype(BF16), frame_head)) for x in t]
    for _ in range(c.bit_length() - 3):
        both = [_dot(jnp.concatenate([t[j], m[j]], axis=0).astype(BF16),
                     _blockdiag(m[j].astype(BF16), frame_head)) for j in n]
        t = [t[j] + m[j] + both[j][:c] for j in n]
        m = [both[j][c:] for j in n]
    t = [(t[j] + m[j] + _dot(t[j].astype(BF16), _blockdiag(m[j].astype(BF16), frame_head))).astype(BF16)
         for j in n]
    for u, tj in zip(units, t):
        kap_b, kv_top = u["lhs"][:c], u["kv"][:c]
        khat = kap_b.astype(F32) + _dot(tj, _blockdiag(kap_b, feat_head))
        u["lhs"] = jnp.concatenate([khat.astype(BF16), u["lhs"][c:]], axis=0)
        u["tkv"] = kv_top + _dot(tj, _blockdiag(kv_top.astype(BF16), feat_head))
    return [[u for u in units if u["chunk"] == ci] for ci in range(len(chunks))]


def _rwkv_advance(s_ref, units, c, n_seq):
    feat_head = _feat_head()
    state_mask = (lax.broadcasted_iota(jnp.int32, (MXU_WIDTH, MXU_WIDTH), 0) // A_HEAD_DIM
                  == lax.broadcasted_iota(jnp.int32, (MXU_WIDTH, MXU_WIDTH), 1) // A_HEAD_DIM)
    s0 = [s_ref[u["seq"], u["group"]] for u in units]
    p = [_dot_nt(u["lhs"], s.astype(BF16)) for u, s in zip(units, s0)]
    uu = [pj[:c] + u["tkv"] for u, pj in zip(units, p)]
    ys = [pj[c:] + u["kv"][c:] - _dot(u["q_b"], _blockdiag(uj.astype(BF16), feat_head))
          for u, pj, uj in zip(units, p, uu)]
    for u, s, uj in zip(units, s0, uu):
        z = jnp.concatenate([u["v"], -uj], axis=0).astype(BF16)
        s_ref[u["seq"], u["group"]] = s * u["gam"] + jnp.where(state_mask, _dot_tn(z, u["kall"]), 0.0)
    return [jnp.concatenate([y for u, y in zip(units, ys) if u["seq"] == i], axis=1) for i in range(n_seq)]


def _even_kernel(xs_ref, wkv0_ref, w0_ref, a0_ref, wlr_ref, kk_ref, ka_ref, rk_ref, lnw_ref, lnb_ref,
                 bd_ref, y_ref, wkv_ref, s_ref, *, bg, tb, c):
    t_idx = pl.program_id(1)

    @pl.when(t_idx == 0)
    def _():
        s_ref[...] = jnp.zeros_like(s_ref)
        for i in range(bg):
            for h in range(A_HEADS):
                q, d = _head_block(h)
                s_ref[i, q, d, d] = wkv0_ref[i, h]

    bd = bd_ref[...]
    inv_n = 1.0 / A_HEAD_DIM
    seqs = range(bg)
    stack = lambda xs: jnp.concatenate(xs, axis=0)
    chunks, keep = [], []
    for ci in range(tb // c):
        rows = slice(ci * c, (ci + 1) * c)
        xs = [xs_ref[i, rows, :] for i in seqs]
        lo_act = stack([x[:, 3 * D_A:] for x in xs])
        z = _dot(lo_act.astype(BF16), wlr_ref[...])
        r = stack([x[:, :D_A] for x in xs])
        k = stack([x[:, D_A:2 * D_A] for x in xs])
        v = stack([x[:, 2 * D_A:3 * D_A] for x in xs])
        log_decay = jax.nn.sigmoid(w0_ref[...] + z[:, :D_A]) * (-DECAY_SCALE)
        a = jax.nn.sigmoid(a0_ref[...] + z[:, D_A:2 * D_A])
        kk = k * kk_ref[...]
        kk = kk * jnp.minimum(lax.rsqrt(_head_sums(kk * kk, bd, 1)), 1.0 / L2_EPS)
        k_eff = k * (1.0 + (a - 1.0) * ka_ref[...])
        b = kk * a
        chunks.append([tuple(x[i * c:(i + 1) * c] for x in (r, log_decay, k_eff, v, kk, b)) for i in seqs])
        keep.append((r * k_eff * rk_ref[...], v, z[:, 2 * D_A:]))
    prepared = _rwkv_prepare(chunks, c)
    for ci, (rk, v, gate) in enumerate(keep):
        rows = slice(ci * c, (ci + 1) * c)
        y = stack(_rwkv_advance(s_ref, prepared[ci], c, bg))
        d = y - _head_sums(y, bd, 1) * inv_n
        var = _head_sums(d * d, bd, 1) * inv_n
        yn = d * lax.rsqrt(var + GN_EPS) * lnw_ref[...] + lnb_ref[...]
        bonus = _head_sums(rk, bd, 1) * v
        y_a = ((yn + bonus) * gate).astype(BF16)
        for i in seqs:
            y_ref[i, rows, :] = y_a[i * c:(i + 1) * c]

    @pl.when(t_idx == pl.num_programs(1) - 1)
    def _():
        for i in range(bg):
            for h in range(A_HEADS):
                q, d = _head_block(h)
                wkv_ref[i, h] = s_ref[i, q, d, d]


def _even_call(xs, wkv0, prm, bg, tb, c):
    bsz, t, _ = xs.shape
    params = (prm["w0"], prm["a0"], prm["w_lr"], prm["k_k"], prm["k_a"], prm["r_k"],
              prm["ln_w"], prm["ln_b"], prm["bd"])
    state_spec = pl.BlockSpec((bg, A_HEADS, A_HEAD_DIM, A_HEAD_DIM), lambda b, i: (b, 0, 0, 0))
    return pl.pallas_call(
        functools.partial(_even_kernel, bg=bg, tb=tb, c=c),
        grid=(bsz // bg, t // tb),
        in_specs=[pl.BlockSpec((bg, tb, A_COLS), lambda b, i: (b, i, 0)), state_spec]
        + [pl.BlockSpec(a.shape, lambda b, i, nd=a.ndim: (0,) * nd) for a in params],
        out_specs=[pl.BlockSpec((bg, tb, D_A), lambda b, i: (b, i, 0)), state_spec],
        out_shape=[jax.ShapeDtypeStruct((bsz, t, D_A), BF16),
                   jax.ShapeDtypeStruct((bsz, A_HEADS, A_HEAD_DIM, A_HEAD_DIM), F32)],
        scratch_shapes=[pltpu.VMEM((bg, D_A // MXU_WIDTH, MXU_WIDTH, MXU_WIDTH), F32)],
        compiler_params=pltpu.CompilerParams(dimension_semantics=("arbitrary", "arbitrary"),
                                             vmem_limit_bytes=VMEM_LIMIT),
        name="even_mixer",
    )(xs, wkv0, *params)


def _odd_kernel(proj_ref, gla0_ref, aup_ref, ab_ref, gn_ref, y_ref, gla_ref, st_ref, *, bg, tb, c):
    t_idx = pl.program_id(1)

    @pl.when(t_idx == 0)
    def _():
        st_ref[...] = gla0_ref[...]

    _, incl = _tril_masks(c)
    tri = jnp.where(incl, 1.0, 0.0).astype(BF16)
    gn = gn_ref[...]
    scale = C_DK ** -0.5
    units = []
    for i in range(bg):
        a_lo = proj_ref[i, :, ODD_MAIN:]
        la = _log_sigmoid(_dot(a_lo.astype(BF16), aup_ref[...]) + ab_ref[...]) * (1.0 / GLA_GATE_NORM)
        for ci in range(tb // c):
            rows = slice(ci * c, (ci + 1) * c)
            cum = _cumsum_rows(tri, la[rows])
            ctot = cum[c - 1:c, :]
            k = proj_ref[i, rows, C_KEY:2 * C_KEY]
            qd = (proj_ref[i, rows, :C_KEY] * scale * jnp.exp(cum)).astype(BF16)
            kd = (k * jnp.exp(-cum)).astype(BF16)
            k2e = (k * jnp.exp(ctot - cum)).astype(BF16)
            gam = jnp.exp(ctot)
            for h in range(C_HEADS):
                ks = slice(h * C_DK, (h + 1) * C_DK)
                cols = slice(2 * C_KEY + h * C_DV, 2 * C_KEY + (h + 1) * C_DV)
                units.append(dict(seq=i, chunk=ci, head=h, rows=rows, qd=qd[:, ks], kd=kd[:, ks],
                                  k2e=k2e[:, ks], gam=gam[:, ks],
                                  v=proj_ref[i, rows, cols].astype(BF16)))
    scores = [jnp.where(incl, _dot_nt(u["qd"], u["kd"]), 0.0).astype(BF16) for u in units]
    intra = [_dot(s, u["v"]) for s, u in zip(scores, units)]
    upd = [_dot_tn(u["v"], u["k2e"]) for u in units]
    state = {(i, h): st_ref[i, h] for i in range(bg) for h in range(C_HEADS)}
    inter = []
    for ci in range(tb // c):
        for u, up in zip(units, upd):
            if u["chunk"] == ci:
                key = (u["seq"], u["head"])
                inter.append((u, _dot_nt(u["qd"], state[key].astype(BF16))))
                state[key] = state[key] * u["gam"] + up
    for key, s in state.items():
        st_ref[key] = s
    inter = {(u["seq"], u["chunk"], u["head"]): x for u, x in inter}
    for u, a in zip(units, intra):
        o = a + inter[(u["seq"], u["chunk"], u["head"])]
        o = o * lax.rsqrt(jnp.mean(o * o, axis=-1, keepdims=True) + RMS_EPS) * gn
        gcols = slice(2 * C_KEY + C_VAL + u["head"] * C_DV, 2 * C_KEY + C_VAL + (u["head"] + 1) * C_DV)
        gate = proj_ref[u["seq"], u["rows"], gcols]
        y_ref[u["seq"], u["rows"], u["head"] * C_DV:(u["head"] + 1) * C_DV] = (o * gate).astype(BF16)

    @pl.when(t_idx == pl.num_programs(1) - 1)
    def _():
        gla_ref[...] = st_ref[...]


def _odd_call(proj, gla0_t, prm, bg, tb, c):
    bsz, t, _ = proj.shape
    params = (prm["a_up"], prm["a_b"], prm["gn"])
    state_spec = pl.BlockSpec((bg, C_HEADS, C_DV, C_DK), lambda b, i: (b, 0, 0, 0))
    return pl.pallas_call(
        functools.partial(_odd_kernel, bg=bg, tb=tb, c=c),
        grid=(bsz // bg, t // tb),
        in_specs=[pl.BlockSpec((bg, tb, ODD_IN_PAD), lambda b, i: (b, i, 0)), state_spec]
        + [pl.BlockSpec(a.shape, lambda b, i, nd=a.ndim: (0,) * nd) for a in params],
        out_specs=[pl.BlockSpec((bg, tb, D_MODEL), lambda b, i: (b, i, 0)), state_spec],
        out_shape=[jax.ShapeDtypeStruct((bsz, t, D_MODEL), BF16),
                   jax.ShapeDtypeStruct((bsz, C_HEADS, C_DV, C_DK), F32)],
        scratch_shapes=[pltpu.VMEM((bg, C_HEADS, C_DV, C_DK), F32)],
        compiler_params=pltpu.CompilerParams(dimension_semantics=("arbitrary", "arbitrary"),
                                             vmem_limit_bytes=VMEM_LIMIT),
        name="odd_mixer",
    )(proj, gla0_t, *params)


def _prepare(w):
    depth = w["ffn_norm"].shape[0]
    row = lambda a: a.reshape(1, -1).astype(F32)

    wg, wu, wd = (w[k].astype(BF16) for k in ("ffn_w_gate", "ffn_w_up", "ffn_w_down"))
    ev_in, ev_out, od_out = (w[k].astype(BF16) for k in ("ev_w_in", "ev_w_out", "od_w_out"))
    od_in = jnp.pad(w["od_w_in"], ((0, 0), (0, 0), (0, ODD_IN_PAD - w["od_w_in"].shape[-1]))).astype(BF16)

    def ffn(layer, j):
        return (row(w["ffn_norm"][layer, j]), (wg, (layer, j)), (wu, (layer, j)), (wd, (layer, j)))

    head = jnp.arange(MXU_WIDTH) // A_HEAD_DIM
    bd = (head[:, None] == head[None, :]).astype(BF16)
    layers = []
    for layer in range(depth):
        i = layer // 2
        if layer % 2 == 0:
            w_lr = jnp.zeros((LOW_RANK, 3 * D_A), F32)
            w_lr = w_lr.at[:DECAY_RANK, :D_A].set(w["rwkv_w_up"][i])
            w_lr = w_lr.at[DECAY_RANK:DECAY_RANK + ICL_RANK, D_A:2 * D_A].set(w["rwkv_a_up"][i])
            w_lr = w_lr.at[DECAY_RANK + ICL_RANK:, 2 * D_A:].set(w["rwkv_g_up"][i])
            mix = dict(mu=row(w["rwkv_mu"][i]), w0=row(w["rwkv_w0"][i]), a0=row(w["rwkv_a0"][i]),
                       w_lr=w_lr.astype(BF16), k_k=row(w["rwkv_k_k"][i]), k_a=row(w["rwkv_k_a"][i]),
                       r_k=row(w["rwkv_r_k"][i]), ln_w=row(w["rwkv_ln_w"][i]), ln_b=row(w["rwkv_ln_b"][i]),
                       conv_w=w["conv_w"][i].astype(F32), bd=bd)
            w_in, w_out = (ev_in, (i,)), (ev_out, (i,))
        else:
            a_up = jnp.zeros((LANES, C_KEY), F32).at[:GLA_GATE_RANK].set(w["gla_a_up"][i])
            mix = dict(a_up=a_up.astype(BF16), a_b=row(w["gla_a_b"][i]), gn=row(w["gla_norm"][i]))
            w_in, w_out = (od_in, (i,)), (od_out, (i,))
        layers.append(dict(ffn0=ffn(layer, 0), ffn1=ffn(layer, 1), gm=row(w["mix_norm"][layer]),
                           w_in=w_in, w_out=w_out, mix=mix))
    return layers, row(w["final_norm"])


def _trunk(x, shift0, wkv0, conv0, gla0, layers, final_norm, tm, bg, tb, c):
    bsz, t, _ = x.shape
    n = bsz * t
    x = x.reshape(n, D_MODEL)
    new_shift, new_wkv, new_conv, new_gla = [], [], [], []
    for layer, lw in enumerate(layers):
        i = layer // 2
        if layer % 2 == 0:
            x, xs, y_b, sh, cv = _pre_even_call(x, lw["ffn0"], lw["gm"], lw["w_in"], shift0[i][:, None, :],
                                                conv0[i], lw["mix"]["mu"], lw["mix"]["conv_w"], tm, t)
            y_a, wkv = _even_call(xs.reshape(bsz, t, A_COLS), wkv0[i], lw["mix"], bg, tb, c)
            ys = [y_a.reshape(n, D_A), y_b]
            new_shift.append(sh[:, 0, :])
            new_wkv.append(wkv)
            new_conv.append(cv)
        else:
            x, proj = _pre_call(x, lw["ffn0"], lw["gm"], lw["w_in"], tm)
            mix, st = _odd_call(proj.reshape(bsz, t, -1), jnp.swapaxes(gla0[i], -1, -2), lw["mix"],
                                bg, tb, c)
            ys = [mix.reshape(n, D_MODEL)]
            new_gla.append(jnp.swapaxes(st, -1, -2))
        tm_post = 2 * tm if tm >= 512 and n % (2 * tm) == 0 else tm
        x = _post_call(x, ys, lw["w_out"], lw["ffn1"], final_norm, tm_post,
                       final=layer == len(layers) - 1)
    return (x.reshape(bsz, t, D_MODEL), jnp.stack(new_shift), jnp.stack(new_wkv),
            jnp.stack(new_conv), jnp.stack(new_gla))


def _tiles(bsz, t):
    n = bsz * t
    tm = 512 if t % 512 == 0 else n
    bg = 2 if bsz % 2 == 0 else 1
    tb = 512 if t % 512 == 0 else t
    c = 64 if tb % 64 == 0 else tb
    return tm, bg, tb, c


def kernel(x_prompt, x_sample, state_rwkv_shift, state_rwkv_wkv, state_conv, state_gla, ffn_norm, ffn_w_gate, ffn_w_up, ffn_w_down, mix_norm, ev_w_in, ev_w_out, rwkv_mu, rwkv_w0, rwkv_w_up, rwkv_a0, rwkv_a_up, rwkv_g_up, rwkv_k_k, rwkv_k_a, rwkv_r_k, rwkv_ln_w, rwkv_ln_b, conv_w, od_w_in, od_w_out, gla_a_up, gla_a_b, gla_norm, final_norm):
    w = dict(ffn_norm=ffn_norm, ffn_w_gate=ffn_w_gate, ffn_w_up=ffn_w_up, ffn_w_down=ffn_w_down,
             mix_norm=mix_norm, ev_w_in=ev_w_in, ev_w_out=ev_w_out, rwkv_mu=rwkv_mu, rwkv_w0=rwkv_w0,
             rwkv_w_up=rwkv_w_up, rwkv_a0=rwkv_a0, rwkv_a_up=rwkv_a_up, rwkv_g_up=rwkv_g_up,
             rwkv_k_k=rwkv_k_k, rwkv_k_a=rwkv_k_a, rwkv_r_k=rwkv_r_k.reshape(rwkv_r_k.shape[0], -1),
             rwkv_ln_w=rwkv_ln_w, rwkv_ln_b=rwkv_ln_b, conv_w=conv_w, od_w_in=od_w_in,
             od_w_out=od_w_out, gla_a_up=gla_a_up, gla_a_b=gla_a_b, gla_norm=gla_norm,
             final_norm=final_norm)
    layers, fnorm = _prepare(w)
    n_even, n_odd = state_rwkv_shift.shape[0], state_gla.shape[0]
    bp = x_prompt.shape[0]
    zeros = lambda *s: jnp.zeros(s, F32)
    y_p, p_shift, p_wkv, p_conv, p_gla = _trunk(
        x_prompt, zeros(n_even, bp, A_COLS), zeros(n_even, bp, A_HEADS, A_HEAD_DIM, A_HEAD_DIM),
        zeros(n_even, bp, CONV_W - 1, D_B), zeros(n_odd, bp, C_HEADS, C_DK, C_DV),
        layers, fnorm, *_tiles(*x_prompt.shape[:2]))
    y_s, s_shift, s_wkv, s_conv, s_gla = _trunk(
        x_sample, state_rwkv_shift, state_rwkv_wkv, state_conv, state_gla,
        layers, fnorm, *_tiles(*x_sample.shape[:2]))
    return (y_p, y_s, p_shift, p_wkv, p_conv, p_gla, s_shift, s_wkv, s_conv, s_gla)
```
